```python
import math
import jax, jax.numpy as jnp
from jax import lax
import numpy as np

D_MODEL = 1024
BATCH = 8
SEQ = 2048
DEPTH = 2
DEC_BATCH = 16
DEC_SEQ = 4096
PAST_LEN = 128

N_HEADS = 16
N_KV_HEADS = 4
HEAD_DIM = 64
Q_PER_KV = N_HEADS // N_KV_HEADS
ATTN_WIDTH = N_HEADS * HEAD_DIM
KV_WIDTH = N_KV_HEADS * HEAD_DIM
WINDOW = 128
BLOCK = 128
NUM_BUCKETS = 32
MAX_DISTANCE = 128
SSD_EXPAND = 2
D_INNER = SSD_EXPAND * D_MODEL
SSD_HEAD_DIM = 64
SSD_HEADS = D_INNER // SSD_HEAD_DIM
SSD_GROUPS = 4
HEADS_PER_GROUP = SSD_HEADS // SSD_GROUPS
D_STATE = 128
CONV_K = 5
CONV_CH = D_INNER + 2 * SSD_GROUPS * D_STATE
CHUNK = 128
MEM_LEN = 256
CROSS_HEADS = 4
CROSS_HEAD_DIM = D_MODEL // CROSS_HEADS
D_FF = ((8 * D_MODEL // 3 + 255) // 256) * 256
IN_WIDTH = ATTN_WIDTH + 2 * KV_WIDTH + D_INNER + CONV_CH + 2 * SSD_HEADS + 2 * D_MODEL
SPLITS = (ATTN_WIDTH,
          ATTN_WIDTH + KV_WIDTH,
          ATTN_WIDTH + 2 * KV_WIDTH,
          ATTN_WIDTH + 2 * KV_WIDTH + D_INNER,
          ATTN_WIDTH + 2 * KV_WIDTH + D_INNER + CONV_CH,
          ATTN_WIDTH + 2 * KV_WIDTH + D_INNER + CONV_CH + 2 * SSD_HEADS)
EPS = 1e-6

kernel_name = 'hybrid_swa_ssd_encoder'


def rmsnorm(x, g):
    xf = x.astype(jnp.float32)
    y = xf * lax.rsqrt(jnp.mean(xf * xf, axis=-1, keepdims=True) + EPS)
    return (y * g.astype(jnp.float32)).astype(x.dtype)


def t5_bucket(rel):
    nb = NUM_BUCKETS // 2
    max_exact = nb // 2
    ret = jnp.where(rel > 0, nb, 0)
    n = jnp.abs(rel)
    nf = jnp.maximum(n, 1).astype(jnp.float32)
    large = max_exact + (jnp.log(nf / max_exact) / math.log(MAX_DISTANCE / max_exact)
                         * (nb - max_exact)).astype(jnp.int32)
    large = jnp.minimum(large, nb - 1)
    return ret + jnp.where(n < max_exact, n, large)


def window_attention(q, k, v, sink, rel_bias):
    b, s = q.shape[0], q.shape[1]
    nblk = s // BLOCK
    span = BLOCK + 2 * WINDOW
    kp = jnp.pad(k, ((0, 0), (WINDOW, WINDOW), (0, 0), (0, 0)))
    vp = jnp.pad(v, ((0, 0), (WINDOW, WINDOW), (0, 0), (0, 0)))
    rel = jnp.arange(span)[None, :] - WINDOW - jnp.arange(BLOCK)[:, None]
    in_window = jnp.abs(rel) <= WINDOW
    bias = jnp.transpose(rel_bias[t5_bucket(rel)], (2, 0, 1))
    bias = bias.reshape(N_KV_HEADS, Q_PER_KV, BLOCK, span).astype(jnp.float32)
    sink_l = sink.reshape(N_KV_HEADS, Q_PER_KV, 1, 1).astype(jnp.float32)
    scale = HEAD_DIM ** -0.5

    def block_fn(i):
        start = i * BLOCK
        qb = lax.dynamic_slice_in_dim(q, start, BLOCK, axis=1)
        kb = lax.dynamic_slice_in_dim(kp, start, span, axis=1)
        vb = lax.dynamic_slice_in_dim(vp, start, span, axis=1)
        key_pos = start - WINDOW + jnp.arange(span)
        valid = in_window & ((key_pos >= 0) & (key_pos < s))[None, :]
        logits = jnp.einsum('bqgrd,bkgd->bgrqk', qb, kb).astype(jnp.float32) * scale + bias
        logits = jnp.where(valid, logits, -jnp.inf)
        m = jnp.maximum(jnp.max(logits, axis=-1, keepdims=True), sink_l)
        e = jnp.exp(logits - m)
        denom = jnp.sum(e, axis=-1, keepdims=True) + jnp.exp(sink_l - m)
        p = (e / denom).astype(v.dtype)
        return jnp.einsum('bgrqk,bkgd->bqgrd', p, vb)

    out = lax.map(block_fn, jnp.arange(nblk))
    return jnp.moveaxis(out, 0, 1).reshape(b, s, ATTN_WIDTH)


def centred_dwconv(u, w, bias):
    pad = CONV_K // 2
    out = lax.conv_general_dilated(u, w[:, None, :].astype(u.dtype), window_strides=(1,),
                                   padding=[(pad, pad)], dimension_numbers=('NWC', 'WIO', 'NWC'),
                                   feature_group_count=u.shape[-1])
    return out + bias.astype(u.dtype)


def segsum(a):
    t = a.shape[-1]
    cs = jnp.cumsum(a, axis=-1)
    diff = cs[..., :, None] - cs[..., None, :]
    mask = jnp.tril(jnp.ones((t, t), dtype=bool))
    return jnp.where(mask, diff, -jnp.inf)


def ssd_scan(xh, dt, a, bm, cm):
    b, l, g, r, p = xh.shape
    n = bm.shape[-1]
    c = l // CHUNK
    xd = (xh * dt[..., None]).reshape(b, c, CHUNK, g, r, p)
    adt = jnp.moveaxis((dt * a).reshape(b, c, CHUNK, g, r), 2, -1)
    bc = bm.reshape(b, c, CHUNK, g, n)
    cc = cm.reshape(b, c, CHUNK, g, n)
    a_cum = jnp.cumsum(adt, axis=-1)
    lmat = jnp.exp(segsum(adt))
    y_diag = jnp.einsum('bclgn,bcsgn,bcgrls,bcsgrp->bclgrp', cc, bc, lmat, xd)
    decay_states = jnp.exp(a_cum[..., -1:] - a_cum)
    states = jnp.einsum('bclgn,bcgrl,bclgrp->bcgrpn', bc, decay_states, xd)
    chunk_a = jnp.moveaxis(a_cum[..., -1], 1, -1)
    decay_chunk = jnp.exp(segsum(jnp.pad(chunk_a, ((0, 0), (0, 0), (0, 0), (1, 0)))))
    states = jnp.pad(states, ((0, 0), (1, 0), (0, 0), (0, 0), (0, 0), (0, 0)))
    states = jnp.einsum('bgrzc,bcgrpn->bzgrpn', decay_chunk, states)[:, :-1]
    y_off = jnp.einsum('bclgn,bcgrpn,bcgrl->bclgrp', cc, states, jnp.exp(a_cum))
    return (y_diag + y_off).reshape(b, l, g, r, p)


def ssd_mixer(xbc, z, dt_raw, conv_w, conv_b, dt_bias, a_log, d_skip, g_norm):
    b, l = xbc.shape[0], xbc.shape[1]
    xbc = jax.nn.silu(centred_dwconv(xbc, conv_w, conv_b))
    xs = xbc[..., :D_INNER]
    bm = xbc[..., D_INNER:D_INNER + SSD_GROUPS * D_STATE].reshape(b, l, SSD_GROUPS, D_STATE)
    cm = xbc[..., D_INNER + SSD_GROUPS * D_STATE:].reshape(b, l, SSD_GROUPS, D_STATE)
    xh = xs.reshape(b, l, SSD_GROUPS, HEADS_PER_GROUP, SSD_HEAD_DIM)
    dt = jax.nn.softplus(dt_raw.reshape(b, l, 2, SSD_HEADS).astype(jnp.float32)
                         + dt_bias.astype(jnp.float32))
    a = -jnp.exp(a_log.astype(jnp.float32))
    dt_f = dt[:, :, 0].reshape(b, l, SSD_GROUPS, HEADS_PER_GROUP)
    dt_b = dt[:, :, 1].reshape(b, l, SSD_GROUPS, HEADS_PER_GROUP)
    a_f = a[0].reshape(SSD_GROUPS, HEADS_PER_GROUP)
    a_b = a[1].reshape(SSD_GROUPS, HEADS_PER_GROUP)
    y_f = ssd_scan(xh, dt_f, a_f, bm, cm)
    y_b = jnp.flip(ssd_scan(jnp.flip(xh, 1), jnp.flip(dt_b, 1), a_b,
                            jnp.flip(bm, 1), jnp.flip(cm, 1)), 1)
    y = y_f + y_b + xh * d_skip.reshape(SSD_GROUPS, HEADS_PER_GROUP, 1)
    y = y.reshape(b, l, D_INNER).astype(xbc.dtype)
    return rmsnorm(y * jax.nn.silu(z), g_norm)


def memory_cross_attention(h, mem, g_mem, w_q, w_kv, w_o):
    b, s = h.shape[0], h.shape[1]
    mh = rmsnorm(mem, g_mem)
    q = (h @ w_q).reshape(b, s, CROSS_HEADS, CROSS_HEAD_DIM)
    kv = mh @ w_kv
    k = kv[..., :D_MODEL].reshape(b, -1, CROSS_HEADS, CROSS_HEAD_DIM)
    v = kv[..., D_MODEL:].reshape(b, -1, CROSS_HEADS, CROSS_HEAD_DIM)
    logits = jnp.einsum('bshd,bmhd->bhsm', q, k).astype(jnp.float32) * CROSS_HEAD_DIM ** -0.5
    p = jax.nn.softmax(logits, axis=-1).astype(v.dtype)
    o = jnp.einsum('bhsm,bmhd->bshd', p, v).reshape(b, s, D_MODEL)
    return o @ w_o


def encoder_layer(x, mem, rel_bias, norm_mix, w_in, attn_sink, conv_w, conv_b, dt_bias, a_log,
                  d_skip, norm_ssd, w_proj_attn, w_proj_ssd, w_out, norm_cross, norm_mem,
                  w_q_cross, w_kv_cross, w_o_cross, norm_ffn, w_gate_up, w_down):
    b, s = x.shape[0], x.shape[1]
    h = rmsnorm(x, norm_mix)
    proj = h @ w_in
    q, k, v, z, xbc, dt_raw, gates = jnp.split(proj, SPLITS, axis=-1)
    q = q.reshape(b, s, N_KV_HEADS, Q_PER_KV, HEAD_DIM)
    k = k.reshape(b, s, N_KV_HEADS, HEAD_DIM)
    v = v.reshape(b, s, N_KV_HEADS, HEAD_DIM)
    attn_o = window_attention(q, k, v, attn_sink, rel_bias)
    ssd_o = ssd_mixer(xbc, z, dt_raw, conv_w, conv_b, dt_bias, a_log, d_skip, norm_ssd)
    gates = jax.nn.sigmoid(gates)
    mixed = gates[..., :D_MODEL] * (attn_o @ w_proj_attn) + gates[..., D_MODEL:] * (ssd_o @ w_proj_ssd)
    x = x + mixed @ w_out
    x = x + memory_cross_attention(rmsnorm(x, norm_cross), mem, norm_mem, w_q_cross, w_kv_cross, w_o_cross)
    gu = rmsnorm(x, norm_ffn) @ w_gate_up
    x = x + (jax.nn.silu(gu[..., :D_FF]) * gu[..., D_FF:]) @ w_down
    return x


def encoder(x, mem, rel_bias, norm_mix, w_in, attn_sink, conv_w, conv_b, dt_bias, a_log, d_skip,
            norm_ssd, w_proj_attn, w_proj_ssd, w_out, norm_cross, norm_mem, w_q_cross, w_kv_cross,
            w_o_cross, norm_ffn, w_gate_up, w_down, norm_final):
    for l in range(DEPTH):
        x = encoder_layer(x, mem, rel_bias, norm_mix[l], w_in[l], attn_sink[l], conv_w[l], conv_b[l],
                          dt_bias[l], a_log[l], d_skip[l], norm_ssd[l], w_proj_attn[l], w_proj_ssd[l],
                          w_out[l], norm_cross[l], norm_mem[l], w_q_cross[l], w_kv_cross[l],
                          w_o_cross[l], norm_ffn[l], w_gate_up[l], w_down[l])
    return rmsnorm(x, norm_final)


def setup_inputs(seed: int = 0) -> dict:
    key = jax.random.key(seed)
    ks = jax.random.split(key, 32)
    f32 = jnp.float32

    def nrm(k, shape, scale):
        return jax.random.normal(k, shape, f32) * scale

    def gain(k, shape):
        return 1.0 + 0.02 * jax.random.normal(k, shape, f32)

    dt0 = jnp.exp(jax.random.uniform(ks[10], (DEPTH, 2, SSD_HEADS), f32,
                                     minval=math.log(1e-3), maxval=math.log(1e-1)))
    return {
        'x_prompt': nrm(ks[0], (BATCH, SEQ, D_MODEL), 1.0),
        'x_sample': nrm(ks[1], (DEC_BATCH, DEC_SEQ, D_MODEL), 1.0),
        'mem_prompt': nrm(ks[2], (BATCH, MEM_LEN, D_MODEL), 1.0),
        'mem_sample': nrm(ks[3], (DEC_BATCH, MEM_LEN, D_MODEL), 1.0),
        'rel_bias': nrm(ks[4], (NUM_BUCKETS, N_HEADS), 0.5),
        'norm_mix': gain(ks[5], (DEPTH, D_MODEL)),
        'w_in': nrm(ks[6], (DEPTH, D_MODEL, IN_WIDTH), D_MODEL ** -0.5),
        'attn_sink': nrm(ks[7], (DEPTH, N_HEADS), 0.5),
        'conv_w': nrm(ks[8], (DEPTH, CONV_K, CONV_CH), CONV_K ** -0.5),
        'conv_b': nrm(ks[9], (DEPTH, CONV_CH), 0.02),
        'dt_bias': dt0 + jnp.log(-jnp.expm1(-dt0)),
        'a_log': jnp.log(jax.random.uniform(ks[11], (DEPTH, 2, SSD_HEADS), f32, minval=1.0, maxval=16.0)),
        'd_skip': gain(ks[12], (DEPTH, SSD_HEADS)),
        'norm_ssd': gain(ks[13], (DEPTH, D_INNER)),
        'w_proj_attn': nrm(ks[14], (DEPTH, ATTN_WIDTH, D_MODEL), ATTN_WIDTH ** -0.5),
        'w_proj_ssd': nrm(ks[15], (DEPTH, D_INNER, D_MODEL), D_INNER ** -0.5),
        'w_out': nrm(ks[16], (DEPTH, D_MODEL, D_MODEL), D_MODEL ** -0.5),
        'norm_cross': gain(ks[17], (DEPTH, D_MODEL)),
        'norm_mem': gain(ks[18], (DEPTH, D_MODEL)),
        'w_q_cross': nrm(ks[19], (DEPTH, D_MODEL, D_MODEL), D_MODEL ** -0.5),
        'w_kv_cross': nrm(ks[20], (DEPTH, D_MODEL, 2 * D_MODEL), D_MODEL ** -0.5),
        'w_o_cross': nrm(ks[21], (DEPTH, D_MODEL, D_MODEL), D_MODEL ** -0.5),
        'norm_ffn': gain(ks[22], (DEPTH, D_MODEL)),
        'w_gate_up': nrm(ks[23], (DEPTH, D_MODEL, 2 * D_FF), D_MODEL ** -0.5),
        'w_down': nrm(ks[24], (DEPTH, D_FF, D_MODEL), D_FF ** -0.5),
        'norm_final': gain(ks[25], (D_MODEL,)),
    }


def reference(x_prompt, x_sample, mem_prompt, mem_sample, rel_bias, norm_mix, w_in, attn_sink,
              conv_w, conv_b, dt_bias, a_log, d_skip, norm_ssd, w_proj_attn, w_proj_ssd, w_out,
              norm_cross, norm_mem, w_q_cross, w_kv_cross, w_o_cross, norm_ffn, w_gate_up, w_down,
              norm_final):
    y_prompt = encoder(x_prompt, mem_prompt, rel_bias, norm_mix, w_in, attn_sink, conv_w, conv_b,
                       dt_bias, a_log, d_skip, norm_ssd, w_proj_attn, w_proj_ssd, w_out, norm_cross,
                       norm_mem, w_q_cross, w_kv_cross, w_o_cross, norm_ffn, w_gate_up, w_down, norm_final)
    y_sample = encoder(x_sample, mem_sample, rel_bias, norm_mix, w_in, attn_sink, conv_w, conv_b,
                       dt_bias, a_log, d_skip, norm_ssd, w_proj_attn, w_proj_ssd, w_out, norm_cross,
                       norm_mem, w_q_cross, w_kv_cross, w_o_cross, norm_ffn, w_gate_up, w_down, norm_final)
    return (y_prompt, y_sample)
```

```python
import functools
import math

import numpy as np
import jax
import jax.numpy as jnp
from jax import lax
from jax.experimental import pallas as pl
from jax.experimental.pallas import tpu as pltpu

F32 = jnp.float32
BF16 = jnp.bfloat16

D_MODEL = 1024
N_HEADS = 16
N_KV_HEADS = 4
HEAD_DIM = 64
Q_PER_KV = N_HEADS // N_KV_HEADS
ATTN_WIDTH = N_HEADS * HEAD_DIM
KV_WIDTH = N_KV_HEADS * HEAD_DIM
WINDOW = 128
BLOCK = 128
SPAN = BLOCK + 2 * WINDOW
NUM_BUCKETS = 32
MAX_DISTANCE = 128
D_INNER = 2 * D_MODEL
SSD_HEAD_DIM = 64
SSD_HEADS = D_INNER // SSD_HEAD_DIM
SSD_GROUPS = 4
HEADS_PER_GROUP = SSD_HEADS // SSD_GROUPS
GROUP_WIDTH = HEADS_PER_GROUP * SSD_HEAD_DIM
D_STATE = 128
BC_WIDTH = 2 * SSD_GROUPS * D_STATE
CONV_K = 5
CONV_PAD = CONV_K // 2
CHUNK = 128
MEM_LEN = 256
CROSS_HEADS = 4
CROSS_HEAD_DIM = D_MODEL // CROSS_HEADS
D_FF = ((8 * D_MODEL // 3 + 255) // 256) * 256
EPS = 1e-6

PROJ_WIDTH = 2 * D_INNER + 2 * D_MODEL + ATTN_WIDTH + BC_WIDTH + 2 * KV_WIDTH
COL_Z = 0
COL_XS = 1
COL_GATES = 2
COL_Q = (3 * D_INNER) // ATTN_WIDTH
COL_BC = (3 * D_INNER + ATTN_WIDTH) // BC_WIDTH
COL_K = (3 * D_INNER + ATTN_WIDTH + BC_WIDTH) // KV_WIDTH
COL_V = COL_K + 1
DT_PAD = 128

LANES = 128
HALO_ROWS = 16
MATMUL_N_CHUNK = 512
ROW_TILE = 512
NEG_BIG = -1e30
VMEM_LIMIT = 56 * 1024 * 1024


def _rms(x, g):
    return x * lax.rsqrt(jnp.mean(x * x, axis=-1, keepdims=True) + EPS) * g


def _silu(x):
    return x * (1.0 / (1.0 + jnp.exp(-x)))


def _sigmoid(x):
    return 1.0 / (1.0 + jnp.exp(-x))


def _softplus(x):
    return jnp.maximum(x, 0.0) + jnp.log1p(jnp.exp(-jnp.abs(x)))


def _dot(a, b):
    return jnp.dot(a, b, preferred_element_type=F32)


def _dot_nt(a, b):
    return lax.dot_general(a, b, (((1,), (1,)), ((), ())), preferred_element_type=F32)


def _resident(shape):
    nd = len(shape)
    return pl.BlockSpec(shape, lambda *_: (0,) * nd, pipeline_mode=pl.Buffered(1))


def _params(n_axes):
    return pltpu.CompilerParams(dimension_semantics=("arbitrary",) * n_axes,
                                vmem_limit_bytes=VMEM_LIMIT)


def _in_proj_kernel(x_ref, g_ref, w_ref, wdt_ref, o_ref, dt_ref):
    h = _rms(x_ref[...], g_ref[...]).astype(BF16)
    for n0 in range(0, PROJ_WIDTH, MATMUL_N_CHUNK):
        o_ref[:, n0:n0 + MATMUL_N_CHUNK] = _dot(h, w_ref[:, n0:n0 + MATMUL_N_CHUNK]).astype(BF16)
    dt_ref[...] = _dot(h, wdt_ref[...])


def _in_proj(x2d, g, w_main, w_dt, tm):
    m = x2d.shape[0]
    return pl.pallas_call(
        _in_proj_kernel,
        grid=(m // tm,),
        in_specs=[pl.BlockSpec((tm, D_MODEL), lambda i: (i, 0)),
                  _resident((1, D_MODEL)),
                  _resident((D_MODEL, PROJ_WIDTH)),
                  _resident((D_MODEL, DT_PAD))],
        out_specs=[pl.BlockSpec((tm, PROJ_WIDTH), lambda i: (i, 0)),
                   pl.BlockSpec((tm, DT_PAD), lambda i: (i, 0))],
        out_shape=[jax.ShapeDtypeStruct((m, PROJ_WIDTH), BF16),
                   jax.ShapeDtypeStruct((m, DT_PAD), F32)],
        compiler_params=_params(1),
        name="in_proj",
    )(x2d, g, w_main, w_dt)


def _attn_kernel(q_ref, kp_ref, kc_ref, kn_ref, vp_ref, vc_ref, vn_ref, bias_ref, sink_ref, o_ref):
    i = pl.program_id(1)
    nblk = pl.num_programs(1)
    q = q_ref[0] * jnp.asarray(HEAD_DIM ** -0.5, BF16)
    k = jnp.concatenate([kp_ref[0], kc_ref[0], kn_ref[0]], axis=0)
    v = jnp.concatenate([vp_ref[0], vc_ref[0], vn_ref[0]], axis=0)
    col = lax.broadcasted_iota(jnp.int32, (1, SPAN), 1)
    valid = ((col >= WINDOW) | (i > 0)) & ((col < WINDOW + BLOCK) | (i < nblk - 1))
    outs = []
    for g in range(N_KV_HEADS):
        kg = k[:, g * HEAD_DIM:(g + 1) * HEAD_DIM]
        vg = v[:, g * HEAD_DIM:(g + 1) * HEAD_DIM]
        for r in range(Q_PER_KV):
            h = g * Q_PER_KV + r
            s = _dot_nt(q[:, h * HEAD_DIM:(h + 1) * HEAD_DIM], kg) + bias_ref[h]
            s = jnp.where(valid, s, NEG_BIG)
            sink = sink_ref[h]
            m = jnp.maximum(jnp.max(s, axis=-1, keepdims=True), sink)
            e = jnp.exp(s - m)
            den = jnp.sum(e, axis=-1, keepdims=True) + jnp.exp(sink - m)
            outs.append(_dot(e.astype(BF16), vg) * (1.0 / den))
    o_ref[0] = jnp.concatenate(outs, axis=-1).astype(BF16)


def _window_attention(proj3d, bias, sink):
    b, s, _ = proj3d.shape
    nblk = s // BLOCK

    def kv_spec(col, shift):
        return pl.BlockSpec((1, BLOCK, KV_WIDTH),
                            lambda bi, i: (bi, jnp.clip(i + shift, 0, nblk - 1), col))

    return pl.pallas_call(
        _attn_kernel,
        grid=(b, nblk),
        in_specs=[pl.BlockSpec((1, BLOCK, ATTN_WIDTH), lambda bi, i: (bi, i, COL_Q)),
                  kv_spec(COL_K, -1), kv_spec(COL_K, 0), kv_spec(COL_K, 1),
                  kv_spec(COL_V, -1), kv_spec(COL_V, 0), kv_spec(COL_V, 1),
                  _resident((N_HEADS, BLOCK, SPAN)),
                  pl.BlockSpec(memory_space=pltpu.SMEM)],
        out_specs=pl.BlockSpec((1, BLOCK, ATTN_WIDTH), lambda bi, i: (bi, i, 0)),
        out_shape=jax.ShapeDtypeStruct((b, s, ATTN_WIDTH), BF16),
        compiler_params=_params(2),
        name="window_attention",
    )(proj3d, proj3d, proj3d, proj3d, proj3d, proj3d, proj3d, bias, sink)


def _split_hi_lo(v):
    hi = v.astype(BF16)
    return hi, (v - hi.astype(F32)).astype(BF16)


def _ssd_kernel(forward, *refs):
    if forward:
        (xs_c, xs_p, xs_n, bc_c, bc_p, bc_n, dt_ref, cwx_ref, cbx_ref, cwb_ref, cbb_ref, dtb_ref,
         alog_ref, e_ref, yb_ref, z_ref, dskip_ref, gn_ref, out_ref,
         ext_x, ext_bc, xact, bcact, state, ybuf) = refs
    else:
        (xs_c, xs_p, xs_n, bc_c, bc_p, bc_n, dt_ref, cwx_ref, cbx_ref, cwb_ref, cbb_ref, dtb_ref,
         alog_ref, e_ref, out_ref, ext_x, ext_bc, xact, bcact, state) = refs
    c = pl.program_id(1)
    nchunk = pl.num_programs(1)
    chunk = c if forward else nchunk - 1 - c
    lane0 = 0 if forward else SSD_HEADS
    total_row = CHUNK - 1 if forward else 0

    @pl.when(c == 0)
    def _():
        state[...] = jnp.zeros_like(state)

    def conv(cur, prev, nxt, ext, w_ref, b_ref, dst, width):
        ext[0:HALO_ROWS, :] = jnp.where(chunk > 0, prev[0].astype(F32), 0.0)
        ext[HALO_ROWS:HALO_ROWS + CHUNK, :] = cur[0].astype(F32)
        ext[HALO_ROWS + CHUNK:, :] = jnp.where(chunk < nchunk - 1, nxt[0].astype(F32), 0.0)
        for c0 in range(0, width, MATMUL_N_CHUNK):
            cols = slice(c0, c0 + MATMUL_N_CHUNK)
            acc = b_ref[:, cols] + w_ref[0:1, cols] * ext[pl.ds(HALO_ROWS - CONV_PAD, CHUNK), cols]
            for t in range(1, CONV_K):
                acc = acc + w_ref[t:t + 1, cols] * ext[pl.ds(HALO_ROWS - CONV_PAD + t, CHUNK), cols]
            dst[:, cols] = _silu(acc)

    conv(xs_c, xs_p, xs_n, ext_x, cwx_ref, cbx_ref, xact, D_INNER)
    conv(bc_c, bc_p, bc_n, ext_bc, cwb_ref, cbb_ref, bcact, BC_WIDTH)

    dtv = _softplus(dt_ref[0] + dtb_ref[...])
    adt = dtv * (-jnp.exp(alog_ref[...]))
    ii = lax.broadcasted_iota(jnp.int32, (CHUNK, CHUNK), 0)
    jj = lax.broadcasted_iota(jnp.int32, (CHUNK, CHUNK), 1)
    scanned = (jj <= ii) if forward else (jj >= ii)
    tri = scanned.astype(BF16)
    a1 = adt.astype(BF16)
    r1 = adt - a1.astype(F32)
    a2 = r1.astype(BF16)
    a3 = (r1 - a2.astype(F32)).astype(BF16)
    pc = _dot(tri, a1) + _dot(tri, a2) + _dot(tri, a3)
    total = pc[total_row:total_row + 1, :]
    off_scale = jnp.exp(pc)
    st_weight = jnp.exp(total - pc) * dtv

    def expand(v):
        hi, lo = _split_hi_lo(v)
        return _dot(hi, e_ref[...]) + _dot(lo, e_ref[...])

    off_e = expand(off_scale)
    w_e = expand(st_weight)
    decay = off_e[total_row:total_row + 1, :]
    pc_t = pc.T
    dt_t = dtv.T

    for g in range(SSD_GROUPS):
        gcols = slice(g * GROUP_WIDTH, (g + 1) * GROUP_WIDTH)
        bg = bcact[:, g * D_STATE:(g + 1) * D_STATE]
        cg = bcact[:, (SSD_GROUPS + g) * D_STATE:(SSD_GROUPS + g + 1) * D_STATE].astype(BF16)
        gmat = _dot_nt(cg, bg.astype(BF16))
        sg = state[g]
        y_off = _dot(cg, sg.astype(BF16)) * off_e[:, gcols]
        xg = xact[:, gcols]
        ys = []
        for r in range(HEADS_PER_GROUP):
            lane = lane0 + g * HEADS_PER_GROUP + r
            delta = pc[:, lane:lane + 1] - pc_t[lane:lane + 1, :]
            lmat = jnp.exp(jnp.where(scanned, delta, NEG_BIG))
            mh = (gmat * lmat * dt_t[lane:lane + 1, :]).astype(BF16)
            ys.append(_dot(mh, xg[:, r * SSD_HEAD_DIM:(r + 1) * SSD_HEAD_DIM].astype(BF16)))
        y = jnp.concatenate(ys, axis=-1) + y_off
        xw = (xg * w_e[:, gcols]).astype(BF16)
        state[g] = sg * decay[:, gcols] + _dot(bg.T.astype(BF16), xw)
        if forward:
            ybuf[:, gcols] = y + yb_ref[0, :, gcols].astype(F32) + xg * dskip_ref[:, gcols]
        else:
            out_ref[0, :, gcols] = y.astype(BF16)

    if forward:
        gated = ybuf[...] * _silu(z_ref[0].astype(F32))
        out_ref[0] = _rms(gated, gn_ref[...]).astype(BF16)


def _ssd_scan(forward, proj3d, dt3d, consts, extra):
    b, s, _ = proj3d.shape
    nchunk = s // CHUNK
    halo_per_chunk = CHUNK // HALO_ROWS
    n_halo = s // HALO_ROWS

    def cidx(c):
        return c if forward else nchunk - 1 - c

    def cur(width, col):
        return pl.BlockSpec((1, CHUNK, width), lambda bi, c: (bi, cidx(c), col))

    def prev(width, col):
        return pl.BlockSpec((1, HALO_ROWS, width),
                            lambda bi, c: (bi, jnp.maximum(cidx(c) * halo_per_chunk - 1, 0), col))

    def nxt(width, col):
        return pl.BlockSpec((1, HALO_ROWS, width),
                            lambda bi, c: (bi, jnp.minimum((cidx(c) + 1) * halo_per_chunk, n_halo - 1), col))

    in_specs = [cur(D_INNER, COL_XS), prev(D_INNER, COL_XS), nxt(D_INNER, COL_XS),
                cur(BC_WIDTH, COL_BC), prev(BC_WIDTH, COL_BC), nxt(BC_WIDTH, COL_BC),
                pl.BlockSpec((1, CHUNK, DT_PAD), lambda bi, c: (bi, cidx(c), 0)),
                _resident((CONV_K, D_INNER)), _resident((1, D_INNER)),
                _resident((CONV_K, BC_WIDTH)), _resident((1, BC_WIDTH)),
                _resident((1, DT_PAD)), _resident((1, DT_PAD)),
                _resident((DT_PAD, D_INNER))]
    operands = [proj3d] * 6 + [dt3d] + list(consts)
    scratch = [pltpu.VMEM((CHUNK + 2 * HALO_ROWS, D_INNER), F32),
               pltpu.VMEM((CHUNK + 2 * HALO_ROWS, BC_WIDTH), F32),
               pltpu.VMEM((CHUNK, D_INNER), F32),
               pltpu.VMEM((CHUNK, BC_WIDTH), F32),
               pltpu.VMEM((SSD_GROUPS, D_STATE, GROUP_WIDTH), F32)]
    if forward:
        y_bwd, d_skip, g_norm = extra
        in_specs += [pl.BlockSpec((1, CHUNK, D_INNER), lambda bi, c: (bi, c, 0)),
                     pl.BlockSpec((1, CHUNK, D_INNER), lambda bi, c: (bi, c, COL_Z)),
                     _resident((1, D_INNER)), _resident((1, D_INNER))]
        operands += [y_bwd, proj3d, d_skip, g_norm]
        scratch.append(pltpu.VMEM((CHUNK, D_INNER), F32))
    return pl.pallas_call(
        functools.partial(_ssd_kernel, forward),
        grid=(b, nchunk),
        in_specs=in_specs,
        out_specs=pl.BlockSpec((1, CHUNK, D_INNER), lambda bi, c: (bi, cidx(c), 0)),
        out_shape=jax.ShapeDtypeStruct((b, s, D_INNER), BF16),
        scratch_shapes=scratch,
        compiler_params=_params(2),
        name="ssd_forward" if forward else "ssd_backward",
    )(*operands)


def _mem_kv_kernel(mem_ref, g_ref, w_ref, o_ref):
    h = _rms(mem_ref[0], g_ref[...]).astype(BF16)
    o_ref[0] = _dot(h, w_ref[...]).astype(BF16)


def _mem_kv(mem, g, w_kv):
    b = mem.shape[0]
    return pl.pallas_call(
        _mem_kv_kernel,
        grid=(b,),
        in_specs=[pl.BlockSpec((1, MEM_LEN, D_MODEL), lambda i: (i, 0, 0)),
                  _resident((1, D_MODEL)), _resident((D_MODEL, 2 * D_MODEL))],
        out_specs=pl.BlockSpec((1, MEM_LEN, 2 * D_MODEL), lambda i: (i, 0, 0)),
        out_shape=jax.ShapeDtypeStruct((b, MEM_LEN, 2 * D_MODEL), BF16),
        compiler_params=_params(1),
        name="memory_kv",
    )(mem, g, w_kv)


def _merge_cross_kernel(ssd_ref, att_ref, gate_ref, x_ref, kv_ref, wpa_ref, wps_ref, wout_ref,
                        gc_ref, wq_ref, wo_ref, o_ref):
    gates = _sigmoid(gate_ref[...].astype(F32))
    mixed = (gates[:, :D_MODEL] * _dot(att_ref[...], wpa_ref[...])
             + gates[:, D_MODEL:] * _dot(ssd_ref[...], wps_ref[...]))
    x1 = x_ref[...] + _dot(mixed.astype(BF16), wout_ref[...])
    h = _rms(x1, gc_ref[...]).astype(BF16)
    q = (_dot(h, wq_ref[...]) * CROSS_HEAD_DIM ** -0.5).astype(BF16)
    outs = []
    for hh in range(CROSS_HEADS):
        hs = slice(hh * CROSS_HEAD_DIM, (hh + 1) * CROSS_HEAD_DIM)
        vs = slice(D_MODEL + hh * CROSS_HEAD_DIM, D_MODEL + (hh + 1) * CROSS_HEAD_DIM)
        s = _dot_nt(q[:, hs], kv_ref[0, :, hs])
        e = jnp.exp(s - jnp.max(s, axis=-1, keepdims=True))
        den = jnp.sum(e, axis=-1, keepdims=True)
        outs.append(_dot(e.astype(BF16), kv_ref[0, :, vs]) * (1.0 / den))
    o = jnp.concatenate(outs, axis=-1).astype(BF16)
    o_ref[...] = x1 + _dot(o, wo_ref[...])


def _merge_cross(ssd_o, attn_o, proj2d, x2d, kv, w, tm, seq):
    m = x2d.shape[0]
    tiles_per_seq = seq // tm
    return pl.pallas_call(
        _merge_cross_kernel,
        grid=(m // tm,),
        in_specs=[pl.BlockSpec((tm, D_INNER), lambda i: (i, 0)),
                  pl.BlockSpec((tm, ATTN_WIDTH), lambda i: (i, 0)),
                  pl.BlockSpec((tm, 2 * D_MODEL), lambda i: (i, COL_GATES)),
                  pl.BlockSpec((tm, D_MODEL), lambda i: (i, 0)),
                  pl.BlockSpec((1, MEM_LEN, 2 * D_MODEL), lambda i: (i // tiles_per_seq, 0, 0)),
                  _resident((ATTN_WIDTH, D_MODEL)), _resident((D_INNER, D_MODEL)),
                  _resident((D_MODEL, D_MODEL)), _resident((1, D_MODEL)),
                  _resident((D_MODEL, D_MODEL)), _resident((D_MODEL, D_MODEL))],
        out_specs=pl.BlockSpec((tm, D_MODEL), lambda i: (i, 0)),
        out_shape=jax.ShapeDtypeStruct((m, D_MODEL), F32),
        compiler_params=_params(1),
        name="merge_cross",
    )(ssd_o, attn_o, proj2d, x2d, kv, w["w_proj_attn"], w["w_proj_ssd"], w["w_out"],
      w["norm_cross"], w["w_q_cross"], w["w_o_cross"])


def _ffn_chunks():
    chunks, f0 = [], 0
    while f0 < D_FF:
        fw = min(MATMUL_N_CHUNK, D_FF - f0)
        chunks.append((f0, fw))
        f0 += fw
    return chunks


def _ffn_kernel(final_norm, x_ref, g_ref, wgu_ref, wd_ref, gf_ref, o_ref, acc_ref):
    x = x_ref[...]
    h = _rms(x, g_ref[...]).astype(BF16)
    for n, (f0, fw) in enumerate(_ffn_chunks()):
        gate = _dot(h, wgu_ref[:, f0:f0 + fw])
        up = _dot(h, wgu_ref[:, D_FF + f0:D_FF + f0 + fw])
        part = _dot((_silu(gate) * up).astype(BF16), wd_ref[f0:f0 + fw, :])
        if n == 0:
            acc_ref[...] = x + part
        else:
            acc_ref[...] += part
    y = acc_ref[...]
    o_ref[...] = _rms(y, gf_ref[...]) if final_norm else y


def _ffn(x2d, g, w_gate_up, w_down, g_final, final_norm, tm):
    m = x2d.shape[0]
    return pl.pallas_call(
        functools.partial(_ffn_kernel, final_norm),
        grid=(m // tm,),
        in_specs=[pl.BlockSpec((tm, D_MODEL), lambda i: (i, 0)),
                  _resident((1, D_MODEL)),
                  _resident((D_MODEL, 2 * D_FF)), _resident((D_FF, D_MODEL)),
                  _resident((1, D_MODEL))],
        out_specs=pl.BlockSpec((tm, D_MODEL), lambda i: (i, 0)),
        out_shape=jax.ShapeDtypeStruct((m, D_MODEL), F32),
        scratch_shapes=[pltpu.VMEM((tm, D_MODEL), F32)],
        compiler_params=_params(1),
        name="ffn_final" if final_norm else "ffn",
    )(x2d, g, w_gate_up, w_down, g_final)


def _t5_bucket(rel):
    nb = NUM_BUCKETS // 2
    max_exact = nb // 2
    ret = jnp.where(rel > 0, nb, 0)
    n = jnp.abs(rel)
    nf = jnp.maximum(n, 1).astype(F32)
    large = max_exact + (jnp.log(nf / max_exact) / math.log(MAX_DISTANCE / max_exact)
                         * (nb - max_exact)).astype(jnp.int32)
    large = jnp.minimum(large, nb - 1)
    return ret + jnp.where(n < max_exact, n, large)


def _attention_bias(rel_bias):
    rel = jnp.arange(SPAN)[None, :] - WINDOW - jnp.arange(BLOCK)[:, None]
    bias = jnp.transpose(rel_bias[_t5_bucket(rel)], (2, 0, 1)).astype(F32)
    return jnp.where((jnp.abs(rel) <= WINDOW)[None], bias, NEG_BIG)


def _head_expand_matrix(lane0):
    e = np.zeros((DT_PAD, D_INNER), np.float32)
    for h in range(SSD_HEADS):
        e[lane0 + h, h * SSD_HEAD_DIM:(h + 1) * SSD_HEAD_DIM] = 1.0
    return jnp.asarray(e, BF16)


def _prepare_layer(l, p):
    w_in = p["w_in"][l]
    o_q, o_k, o_v = 0, ATTN_WIDTH, ATTN_WIDTH + KV_WIDTH
    o_z = ATTN_WIDTH + 2 * KV_WIDTH
    o_xs = o_z + D_INNER
    o_bc = o_xs + D_INNER
    o_dt = o_bc + BC_WIDTH
    o_g = o_dt + 2 * SSD_HEADS
    w_main = jnp.concatenate([w_in[:, o_z:o_xs], w_in[:, o_xs:o_bc], w_in[:, o_g:o_g + 2 * D_MODEL],
                              w_in[:, o_q:o_k], w_in[:, o_bc:o_dt], w_in[:, o_k:o_v],
                              w_in[:, o_v:o_z]], axis=1).astype(BF16)
    w_dt = jnp.pad(w_in[:, o_dt:o_g], ((0, 0), (0, DT_PAD - 2 * SSD_HEADS))).astype(BF16)
    pad_dt = lambda a: jnp.pad(a.reshape(1, 2 * SSD_HEADS), ((0, 0), (0, DT_PAD - 2 * SSD_HEADS)))
    row = lambda a: a.reshape(1, -1).astype(F32)
    return {
        "norm_mix": row(p["norm_mix"][l]), "w_main": w_main, "w_dt": w_dt,
        "attn_sink": p["attn_sink"][l].astype(F32),
        "conv_w_x": p["conv_w"][l][:, :D_INNER], "conv_b_x": row(p["conv_b"][l][:D_INNER]),
        "conv_w_bc": p["conv_w"][l][:, D_INNER:], "conv_b_bc": row(p["conv_b"][l][D_INNER:]),
        "dt_bias": pad_dt(p["dt_bias"][l]), "a_log": pad_dt(p["a_log"][l]),
        "d_skip": row(jnp.repeat(p["d_skip"][l], SSD_HEAD_DIM)),
        "norm_ssd": row(p["norm_ssd"][l]),
        "w_proj_attn": p["w_proj_attn"][l].astype(BF16), "w_proj_ssd": p["w_proj_ssd"][l].astype(BF16),
        "w_out": p["w_out"][l].astype(BF16), "norm_cross": row(p["norm_cross"][l]),
        "norm_mem": row(p["norm_mem"][l]), "w_q_cross": p["w_q_cross"][l].astype(BF16),
        "w_kv_cross": p["w_kv_cross"][l].astype(BF16), "w_o_cross": p["w_o_cross"][l].astype(BF16),
        "norm_ffn": row(p["norm_ffn"][l]), "w_gate_up": p["w_gate_up"][l].astype(BF16),
        "w_down": p["w_down"][l].astype(BF16),
    }


def _encoder(x, mem, layers, bias, e_fwd, e_bwd, norm_final):
    b, s, _ = x.shape
    m = b * s
    tm = min(ROW_TILE, s)
    x2d = x.reshape(m, D_MODEL)
    for l, w in enumerate(layers):
        proj, dt = _in_proj(x2d, w["norm_mix"], w["w_main"], w["w_dt"], tm)
        proj3d = proj.reshape(b, s, PROJ_WIDTH)
        dt3d = dt.reshape(b, s, DT_PAD)
        attn_o = _window_attention(proj3d, bias, w["attn_sink"])
        consts = [w["conv_w_x"], w["conv_b_x"], w["conv_w_bc"], w["conv_b_bc"], w["dt_bias"], w["a_log"]]
        y_bwd = _ssd_scan(False, proj3d, dt3d, consts + [e_bwd], None)
        ssd_o = _ssd_scan(True, proj3d, dt3d, consts + [e_fwd], (y_bwd, w["d_skip"], w["norm_ssd"]))
        kv = _mem_kv(mem, w["norm_mem"], w["w_kv_cross"])
        x2d = _merge_cross(ssd_o.reshape(m, D_INNER), attn_o.reshape(m, ATTN_WIDTH), proj, x2d, kv, w,
                           tm, s)
        x2d = _ffn(x2d, w["norm_ffn"], w["w_gate_up"], w["w_down"], norm_final,
                   l == len(layers) - 1, tm)
    return x2d.reshape(b, s, D_MODEL)


def kernel(x_prompt, x_sample, mem_prompt, mem_sample, rel_bias, norm_mix, w_in, attn_sink, conv_w,
           conv_b, dt_bias, a_log, d_skip, norm_ssd, w_proj_attn, w_proj_ssd, w_out, norm_cross,
           norm_mem, w_q_cross, w_kv_cross, w_o_cross, norm_ffn, w_gate_up, w_down, norm_final):
    p = dict(norm_mix=norm_mix, w_in=w_in, attn_sink=attn_sink, conv_w=conv_w, conv_b=conv_b,
             dt_bias=dt_bias, a_log=a_log, d_skip=d_skip, norm_ssd=norm_ssd, w_proj_attn=w_proj_attn,
             w_proj_ssd=w_proj_ssd, w_out=w_out, norm_cross=norm_cross, norm_mem=norm_mem,
             w_q_cross=w_q_cross, w_kv_cross=w_kv_cross, w_o_cross=w_o_cross, norm_ffn=norm_ffn,
             w_gate_up=w_gate_up, w_down=w_down)
    layers = [_prepare_layer(l, p) for l in range(norm_mix.shape[0])]
    bias = _attention_bias(rel_bias)
    e_fwd = _head_expand_matrix(0)
    e_bwd = _head_expand_matrix(SSD_HEADS)
    g_final = norm_final.reshape(1, D_MODEL).astype(F32)
    y_prompt = _encoder(x_prompt, mem_prompt, layers, bias, e_fwd, e_bwd, g_final)
    y_sample = _encoder(x_sample, mem_sample, layers, bias, e_fwd, e_bwd, g_final)
    return (y_prompt, y_sample)
```

```python
import functools
import math

import numpy as np
import jax
import jax.numpy as jnp
from jax import lax
from jax.experimental import pallas as pl
from jax.experimental.pallas import tpu as pltpu

F32 = jnp.float32
BF16 = jnp.bfloat16

D_MODEL = 1024
N_HEADS = 16
N_KV_HEADS = 4
HEAD_DIM = 64
Q_PER_KV = N_HEADS // N_KV_HEADS
ATTN_WIDTH = N_HEADS * HEAD_DIM
KV_WIDTH = N_KV_HEADS * HEAD_DIM
WINDOW = 128
BLOCK = 128
SPAN = BLOCK + 2 * WINDOW
NUM_BUCKETS = 32
MAX_DISTANCE = 128
D_INNER = 2 * D_MODEL
SSD_HEAD_DIM = 64
SSD_HEADS = D_INNER // SSD_HEAD_DIM
SSD_GROUPS = 4
HEADS_PER_GROUP = SSD_HEADS // SSD_GROUPS
GROUP_WIDTH = HEADS_PER_GROUP * SSD_HEAD_DIM
D_STATE = 128
BC_WIDTH = 2 * SSD_GROUPS * D_STATE
CONV_K = 5
CONV_PAD = CONV_K // 2
CHUNK = 128
MEM_LEN = 256
CROSS_HEADS = 4
CROSS_HEAD_DIM = D_MODEL // CROSS_HEADS
D_FF = ((8 * D_MODEL // 3 + 255) // 256) * 256
EPS = 1e-6
LOG2_E = math.log2(math.e)

PROJ_WIDTH = 2 * D_INNER + 2 * D_MODEL + ATTN_WIDTH + BC_WIDTH + 2 * KV_WIDTH
COL_Z = 0
COL_XS = 1
COL_GATES = 2
COL_Q = (3 * D_INNER) // ATTN_WIDTH
COL_BC = (3 * D_INNER + ATTN_WIDTH) // BC_WIDTH
COL_K = (3 * D_INNER + ATTN_WIDTH + BC_WIDTH) // KV_WIDTH
COL_V = COL_K + 1
DT_PAD = 128

HALO_ROWS = 16
MATMUL_N_CHUNK = 512
ROW_TILE = 512
NEG_BIG = -1e30
VMEM_LIMIT = 56 * 1024 * 1024


def _rms(x, g):
    return x * lax.rsqrt(jnp.mean(x * x, axis=-1, keepdims=True) + EPS) * g


def _silu(x):
    return x * (1.0 / (1.0 + jnp.exp(-x)))


def _sigmoid(x):
    return 1.0 / (1.0 + jnp.exp(-x))


def _softplus(x):
    return jnp.maximum(x, 0.0) + jnp.log1p(jnp.exp(-jnp.abs(x)))


def _dot(a, b):
    return jnp.dot(a, b, preferred_element_type=F32)


def _dot_nt(a, b):
    return lax.dot_general(a, b, (((1,), (1,)), ((), ())), preferred_element_type=F32)


def _resident(shape):
    nd = len(shape)
    return pl.BlockSpec(shape, lambda *_: (0,) * nd, pipeline_mode=pl.Buffered(1))


def _params(n_axes):
    return pltpu.CompilerParams(dimension_semantics=("arbitrary",) * n_axes,
                                vmem_limit_bytes=VMEM_LIMIT)


def _in_proj_kernel(tiles_per_seq, x_ref, xp_ref, xn_ref, g_ref, w_ref, wdt_ref, cw_ref, cb_ref,
                    o_ref, dt_ref):
    i = pl.program_id(0)
    tm = x_ref.shape[0]
    g = g_ref[...]
    h = _rms(x_ref[...], g).astype(BF16)
    hp = jnp.where(i % tiles_per_seq != 0, _rms(xp_ref[...], g), 0.0)
    hn = jnp.where((i + 1) % tiles_per_seq != 0, _rms(xn_ref[...], g), 0.0)
    halo = jnp.concatenate([hp, hn], axis=0).astype(BF16)
    rows = tm + 2 * HALO_ROWS

    def finish(n0, main, edge):
        conv_col = _conv_column(n0)
        if conv_col is not None:
            ext = jnp.concatenate([edge[:HALO_ROWS], main, edge[HALO_ROWS:]], axis=0)
            ccols = slice(conv_col, conv_col + MATMUL_N_CHUNK)
            acc = cb_ref[:, ccols] + cw_ref[CONV_PAD:CONV_PAD + 1, ccols] * main
            for t in range(CONV_K):
                if t != CONV_PAD:
                    shifted = pltpu.roll(ext, (CONV_PAD - t) % rows, axis=0)[HALO_ROWS:HALO_ROWS + tm]
                    acc = acc + cw_ref[t:t + 1, ccols] * shifted
            main = _silu(acc)
        elif n0 < (COL_Z + 1) * D_INNER:
            main = _silu(main)
        o_ref[:, n0:n0 + MATMUL_N_CHUNK] = main.astype(BF16)

    pending = None
    for n0 in _chunk_order():
        cols = slice(n0, n0 + MATMUL_N_CHUNK)
        main = _dot(h, w_ref[:, cols])
        edge = _dot(halo, w_ref[:, cols]) if _conv_column(n0) is not None else None
        if pending is not None:
            finish(*pending)
        pending = (n0, main, edge)
    dt_ref[...] = _dot(h, wdt_ref[...])
    finish(*pending)


def _chunk_order():
    chunks = list(range(0, PROJ_WIDTH, MATMUL_N_CHUNK))
    heavy = [n0 for n0 in chunks if _conv_column(n0) is not None]
    light = [n0 for n0 in chunks if _conv_column(n0) is None]
    order = []
    while heavy or light:
        if heavy:
            order.append(heavy.pop(0))
        take = -(-len(light) // (len(heavy) + 1))
        order.extend(light[:take])
        light = light[take:]
    return order


def _conv_column(n0):
    if COL_XS * D_INNER <= n0 < (COL_XS + 1) * D_INNER:
        return n0 - COL_XS * D_INNER
    if COL_BC * BC_WIDTH <= n0 < (COL_BC + 1) * BC_WIDTH:
        return D_INNER + n0 - COL_BC * BC_WIDTH
    return None


def _in_proj(x2d, w, tm, seq):
    m = x2d.shape[0]
    halo_per_tile = tm // HALO_ROWS
    n_halo = m // HALO_ROWS
    return pl.pallas_call(
        functools.partial(_in_proj_kernel, seq // tm),
        grid=(m // tm,),
        in_specs=[pl.BlockSpec((tm, D_MODEL), lambda i: (i, 0)),
                  pl.BlockSpec((HALO_ROWS, D_MODEL), lambda i: (jnp.maximum(i * halo_per_tile - 1, 0), 0)),
                  pl.BlockSpec((HALO_ROWS, D_MODEL),
                               lambda i: (jnp.minimum((i + 1) * halo_per_tile, n_halo - 1), 0)),
                  _resident((1, D_MODEL)),
                  _resident((D_MODEL, PROJ_WIDTH)),
                  _resident((D_MODEL, DT_PAD)),
                  _resident((CONV_K, D_INNER + BC_WIDTH)), _resident((1, D_INNER + BC_WIDTH))],
        out_specs=[pl.BlockSpec((tm, PROJ_WIDTH), lambda i: (i, 0)),
                   pl.BlockSpec((tm, DT_PAD), lambda i: (i, 0))],
        out_shape=[jax.ShapeDtypeStruct((m, PROJ_WIDTH), BF16),
                   jax.ShapeDtypeStruct((m, DT_PAD), F32)],
        compiler_params=_params(1),
        name="in_proj",
    )(x2d, x2d, x2d, w["norm_mix"], w["w_main"], w["w_dt"], w["conv_w"], w["conv_b"])


def _attn_kernel(q_ref, kp_ref, kc_ref, kn_ref, vp_ref, vc_ref, vn_ref, bias_ref, sink_ref, o_ref):
    q = q_ref[0] * jnp.asarray(HEAD_DIM ** -0.5, BF16)
    k = jnp.concatenate([kp_ref[0], kc_ref[0], kn_ref[0]], axis=0)
    v = jnp.concatenate([vp_ref[0], vc_ref[0], vn_ref[0]], axis=0)
    v_t = v.astype(F32).T.astype(BF16)
    cols = Q_PER_KV * BLOCK
    ones_rows = (lax.broadcasted_iota(jnp.int32, (HALO_ROWS, SPAN), 0) == 0).astype(BF16)

    def logits(g):
        q4 = jnp.concatenate([q[:, (g * Q_PER_KV + r) * HEAD_DIM:(g * Q_PER_KV + r + 1) * HEAD_DIM]
                              for r in range(Q_PER_KV)], axis=0)
        return _dot_nt(k[:, g * HEAD_DIM:(g + 1) * HEAD_DIM], q4) + bias_ref[0, g * SPAN:(g + 1) * SPAN, :]

    outs = []
    s_next = logits(0)
    for g in range(N_KV_HEADS):
        s = s_next
        if g + 1 < N_KV_HEADS:
            s_next = logits(g + 1)
        sink = sink_ref[:, g * cols:(g + 1) * cols]
        m = jnp.maximum(jnp.max(s, axis=0, keepdims=True), sink)
        e = jnp.exp(s - m).astype(BF16)
        v_aug = jnp.concatenate([v_t[g * HEAD_DIM:(g + 1) * HEAD_DIM], ones_rows], axis=0)
        o_aug = _dot(v_aug, e)
        den = o_aug[HEAD_DIM:HEAD_DIM + 1] + jnp.exp(sink - m)
        o_t = o_aug[:HEAD_DIM] * (1.0 / den)
        outs.extend(o_t[:, r * BLOCK:(r + 1) * BLOCK] for r in range(Q_PER_KV))
    o_ref[0] = jnp.concatenate(outs, axis=0).T.astype(BF16)


def _window_attention(proj3d, bias, sink):
    b, s, _ = proj3d.shape
    nblk = s // BLOCK

    def kv_spec(col, shift):
        return pl.BlockSpec((1, BLOCK, KV_WIDTH),
                            lambda bi, i: (bi, jnp.clip(i + shift, 0, nblk - 1), col))

    def bias_variant(bi, i):
        first = (i == 0).astype(jnp.int32)
        last = (i == nblk - 1).astype(jnp.int32)
        return (first + 2 * last, 0, 0)

    return pl.pallas_call(
        _attn_kernel,
        grid=(b, nblk),
        in_specs=[pl.BlockSpec((1, BLOCK, ATTN_WIDTH), lambda bi, i: (bi, i, COL_Q)),
                  kv_spec(COL_K, -1), kv_spec(COL_K, 0), kv_spec(COL_K, 1),
                  kv_spec(COL_V, -1), kv_spec(COL_V, 0), kv_spec(COL_V, 1),
                  pl.BlockSpec((1, N_KV_HEADS * SPAN, Q_PER_KV * BLOCK), bias_variant),
                  _resident((1, N_HEADS * BLOCK))],
        out_specs=pl.BlockSpec((1, BLOCK, ATTN_WIDTH), lambda bi, i: (bi, i, 0)),
        out_shape=jax.ShapeDtypeStruct((b, s, ATTN_WIDTH), BF16),
        compiler_params=_params(2),
        name="window_attention",
    )(proj3d, proj3d, proj3d, proj3d, proj3d, proj3d, proj3d, bias, sink)


def _attention_call(proj3d, tables, w):
    return _window_attention(proj3d, tables["attn_bias"], w["attn_sink"])


def _split_hi_lo(v):
    hi = v.astype(BF16)
    return hi, (v - hi.astype(F32)).astype(BF16)


def _scan_tables(forward, dt_ref, dtb_ref, alog_ref, e2_ref):
    total_row = CHUNK - 1 if forward else 0
    dtv = _softplus(dt_ref[0] + dtb_ref[...])
    adt = dtv * (-jnp.exp(alog_ref[...])) * LOG2_E
    ii = lax.broadcasted_iota(jnp.int32, (CHUNK, CHUNK), 0)
    jj = lax.broadcasted_iota(jnp.int32, (CHUNK, CHUNK), 1)
    scanned = (jj <= ii) if forward else (jj >= ii)
    tri = scanned.astype(BF16)
    a1 = adt.astype(BF16)
    r1 = adt - a1.astype(F32)
    a2 = r1.astype(BF16)
    a3 = (r1 - a2.astype(F32)).astype(BF16)
    pc = _dot(jnp.concatenate([tri, tri, tri], axis=1), jnp.concatenate([a1, a2, a3], axis=0))
    total = pc[total_row:total_row + 1, :]
    both = jnp.concatenate([jnp.exp2(pc), jnp.exp2(total - pc) * dtv], axis=0)
    hi, lo = _split_hi_lo(both)
    both_e = _dot(jnp.concatenate([hi, lo], axis=1), e2_ref[...])
    off_e = both_e[:CHUNK]
    w_e = both_e[CHUNK:]
    decay = off_e[total_row:total_row + 1, :]
    row_t = (pc - jnp.log(dtv) * LOG2_E).T
    return scanned, pc, row_t, off_e, w_e, decay


def _scan_group(g, lane0, scanned, pc, row_t, off_e, decay, b_t, c_bf, x_bf, xw_bf, state):
    gcols = slice(g * GROUP_WIDTH, (g + 1) * GROUP_WIDTH)
    gmat = _dot(c_bf, b_t)
    sg = state[g]
    y_off = _dot(c_bf, sg.astype(BF16)) * off_e[:, gcols]
    ys = []
    for r in range(HEADS_PER_GROUP):
        lane = lane0 + g * HEADS_PER_GROUP + r
        delta = pc[:, lane:lane + 1] - row_t[lane:lane + 1, :]
        mh = (gmat * jnp.exp2(jnp.where(scanned, delta, NEG_BIG))).astype(BF16)
        ys.append(_dot(mh, x_bf[:, r * SSD_HEAD_DIM:(r + 1) * SSD_HEAD_DIM]))
    state[g] = sg * decay[:, gcols] + _dot(b_t, xw_bf)
    return jnp.concatenate(ys, axis=-1) + y_off


def _ssd_bwd_kernel(x_ref, bc_ref, dt_ref, dtb_ref, alog_ref, e2_ref, y_ref, bt_ref, state):
    @pl.when(pl.program_id(1) == 0)
    def _():
        state[...] = jnp.zeros_like(state)

    scanned, pc, row_t, off_e, w_e, decay = _scan_tables(False, dt_ref, dtb_ref, alog_ref, e2_ref)
    for g in range(SSD_GROUPS):
        gcols = slice(g * GROUP_WIDTH, (g + 1) * GROUP_WIDTH)
        x_bf = x_ref[0, :, gcols]
        b_t = bc_ref[0, :, g * D_STATE:(g + 1) * D_STATE].astype(F32).T.astype(BF16)
        bt_ref[0, g * D_STATE:(g + 1) * D_STATE, :] = b_t
        c_bf = bc_ref[0, :, (SSD_GROUPS + g) * D_STATE:(SSD_GROUPS + g + 1) * D_STATE]
        y = _scan_group(g, SSD_HEADS, scanned, pc, row_t, off_e, decay, b_t, c_bf, x_bf,
                        (x_bf.astype(F32) * w_e[:, gcols]).astype(BF16), state)
        y_ref[0, :, gcols] = y.astype(BF16)


def _ssd_fwd_kernel(x_ref, c_ref, bt_ref, dt_ref, dtb_ref, alog_ref, e2_ref, yb_ref, z_ref, dskip_ref,
                    gn_ref, out_ref, state, ybuf):
    @pl.when(pl.program_id(1) == 0)
    def _():
        state[...] = jnp.zeros_like(state)

    scanned, pc, row_t, off_e, w_e, decay = _scan_tables(True, dt_ref, dtb_ref, alog_ref, e2_ref)
    for g in range(SSD_GROUPS):
        gcols = slice(g * GROUP_WIDTH, (g + 1) * GROUP_WIDTH)
        x_bf = x_ref[0, :, gcols]
        xg = x_bf.astype(F32)
        b_t = bt_ref[0, g * D_STATE:(g + 1) * D_STATE, :]
        c_bf = c_ref[0, :, g * D_STATE:(g + 1) * D_STATE]
        y = _scan_group(g, 0, scanned, pc, row_t, off_e, decay, b_t, c_bf, x_bf,
                        (xg * w_e[:, gcols]).astype(BF16), state)
        ybuf[:, gcols] = y + yb_ref[0, :, gcols].astype(F32) + xg * dskip_ref[:, gcols]
    gated = ybuf[...] * z_ref[0].astype(F32)
    out_ref[0] = _rms(gated, gn_ref[...]).astype(BF16)


def _ssd_call(proj3d, dt3d, tables, w):
    b, s, _ = proj3d.shape
    nchunk = s // CHUNK
    bc_half = BC_WIDTH // 2

    def rev_spec(width, col=0):
        return pl.BlockSpec((1, CHUNK, width), lambda bi, c: (bi, nchunk - 1 - c, col))

    def chunk_spec(width, col=0):
        return pl.BlockSpec((1, CHUNK, width), lambda bi, c: (bi, c, col))

    state = pltpu.VMEM((SSD_GROUPS, D_STATE, GROUP_WIDTH), F32)
    y_bwd, bt = pl.pallas_call(
        _ssd_bwd_kernel,
        grid=(b, nchunk),
        in_specs=[rev_spec(D_INNER, COL_XS), rev_spec(BC_WIDTH, COL_BC), rev_spec(DT_PAD),
                  _resident((1, DT_PAD)), _resident((1, DT_PAD)), _resident((2 * DT_PAD, D_INNER))],
        out_specs=[rev_spec(D_INNER),
                   pl.BlockSpec((1, bc_half, CHUNK), lambda bi, c: (bi, nchunk - 1 - c, 0))],
        out_shape=[jax.ShapeDtypeStruct((b, s, D_INNER), BF16),
                   jax.ShapeDtypeStruct((b, nchunk * bc_half, CHUNK), BF16)],
        scratch_shapes=[state],
        compiler_params=_params(2),
        name="ssd_backward",
    )(proj3d, proj3d, dt3d, w["dt_bias"], w["a_log"], tables["expand_bwd"])

    return pl.pallas_call(
        _ssd_fwd_kernel,
        grid=(b, nchunk),
        in_specs=[chunk_spec(D_INNER, COL_XS), chunk_spec(bc_half, COL_BC * 2 + 1),
                  pl.BlockSpec((1, bc_half, CHUNK), lambda bi, c: (bi, c, 0)), chunk_spec(DT_PAD),
                  _resident((1, DT_PAD)), _resident((1, DT_PAD)), _resident((2 * DT_PAD, D_INNER)),
                  chunk_spec(D_INNER), chunk_spec(D_INNER, COL_Z),
                  _resident((1, D_INNER)), _resident((1, D_INNER))],
        out_specs=chunk_spec(D_INNER),
        out_shape=jax.ShapeDtypeStruct((b, s, D_INNER), BF16),
        scratch_shapes=[state, pltpu.VMEM((CHUNK, D_INNER), F32)],
        compiler_params=_params(2),
        name="ssd_forward",
    )(proj3d, proj3d, bt, dt3d, w["dt_bias"], w["a_log"], tables["expand_fwd"], y_bwd, proj3d,
      w["d_skip"], w["norm_ssd"])


def _mem_kv_kernel(mem_ref, g_ref, w_ref, o_ref):
    h = _rms(mem_ref[0], g_ref[...]).astype(BF16)
    o_ref[0] = _dot(h, w_ref[...]).astype(BF16)


def _mem_kv(mem, g, w_kv):
    b = mem.shape[0]
    return pl.pallas_call(
        _mem_kv_kernel,
        grid=(b,),
        in_specs=[pl.BlockSpec((1, MEM_LEN, D_MODEL), lambda i: (i, 0, 0)),
                  _resident((1, D_MODEL)), _resident((D_MODEL, 2 * D_MODEL))],
        out_specs=pl.BlockSpec((1, MEM_LEN, 2 * D_MODEL), lambda i: (i, 0, 0)),
        out_shape=jax.ShapeDtypeStruct((b, MEM_LEN, 2 * D_MODEL), BF16),
        compiler_params=_params(1),
        name="memory_kv",
    )(mem, g, w_kv)


def _merge_cross_kernel(ssd_ref, att_ref, gate_ref, x_ref, kv_ref, wpa_ref, wps_ref, wout_ref,
                        gc_ref, wq_ref, wo_ref, o_ref):
    gates = _sigmoid(gate_ref[...].astype(F32))
    mixed = (gates[:, :D_MODEL] * _dot(att_ref[...], wpa_ref[...])
             + gates[:, D_MODEL:] * _dot(ssd_ref[...], wps_ref[...]))
    x1 = x_ref[...] + _dot(mixed.astype(BF16), wout_ref[...])
    h = _rms(x1, gc_ref[...]).astype(BF16)
    q = (_dot(h, wq_ref[...]) * CROSS_HEAD_DIM ** -0.5).astype(BF16)
    outs = []
    for hh in range(CROSS_HEADS):
        hs = slice(hh * CROSS_HEAD_DIM, (hh + 1) * CROSS_HEAD_DIM)
        vs = slice(D_MODEL + hh * CROSS_HEAD_DIM, D_MODEL + (hh + 1) * CROSS_HEAD_DIM)
        s = _dot_nt(q[:, hs], kv_ref[0, :, hs])
        e = jnp.exp(s - jnp.max(s, axis=-1, keepdims=True))
        den = jnp.sum(e, axis=-1, keepdims=True)
        outs.append(_dot(e.astype(BF16), kv_ref[0, :, vs]) * (1.0 / den))
    o = jnp.concatenate(outs, axis=-1).astype(BF16)
    o_ref[...] = x1 + _dot(o, wo_ref[...])


def _merge_cross(ssd_o, attn_o, proj2d, x2d, kv, w, tm, seq):
    m = x2d.shape[0]
    tiles_per_seq = seq // tm
    return pl.pallas_call(
        _merge_cross_kernel,
        grid=(m // tm,),
        in_specs=[pl.BlockSpec((tm, D_INNER), lambda i: (i, 0)),
                  pl.BlockSpec((tm, ATTN_WIDTH), lambda i: (i, 0)),
                  pl.BlockSpec((tm, 2 * D_MODEL), lambda i: (i, COL_GATES)),
                  pl.BlockSpec((tm, D_MODEL), lambda i: (i, 0)),
                  pl.BlockSpec((1, MEM_LEN, 2 * D_MODEL), lambda i: (i // tiles_per_seq, 0, 0)),
                  _resident((ATTN_WIDTH, D_MODEL)), _resident((D_INNER, D_MODEL)),
                  _resident((D_MODEL, D_MODEL)), _resident((1, D_MODEL)),
                  _resident((D_MODEL, D_MODEL)), _resident((D_MODEL, D_MODEL))],
        out_specs=pl.BlockSpec((tm, D_MODEL), lambda i: (i, 0)),
        out_shape=jax.ShapeDtypeStruct((m, D_MODEL), F32),
        compiler_params=_params(1),
        name="merge_cross",
    )(ssd_o, attn_o, proj2d, x2d, kv, w["w_proj_attn"], w["w_proj_ssd"], w["w_out"],
      w["norm_cross"], w["w_q_cross"], w["w_o_cross"])


def _ffn_chunks():
    chunks, f0 = [], 0
    while f0 < D_FF:
        fw = min(MATMUL_N_CHUNK, D_FF - f0)
        chunks.append((f0, fw))
        f0 += fw
    return chunks


def _ffn_kernel(final_norm, x_ref, g_ref, wgu_ref, wd_ref, gf_ref, o_ref, acc_ref):
    x = x_ref[...]
    h = _rms(x, g_ref[...]).astype(BF16)
    for n, (f0, fw) in enumerate(_ffn_chunks()):
        gate = _dot(h, wgu_ref[:, f0:f0 + fw])
        up = _dot(h, wgu_ref[:, D_FF + f0:D_FF + f0 + fw])
        part = _dot((_silu(gate) * up).astype(BF16), wd_ref[f0:f0 + fw, :])
        if n == 0:
            acc_ref[...] = x + part
        else:
            acc_ref[...] += part
    y = acc_ref[...]
    o_ref[...] = _rms(y, gf_ref[...]) if final_norm else y


def _ffn(x2d, g, w_gate_up, w_down, g_final, final_norm, tm):
    m = x2d.shape[0]
    return pl.pallas_call(
        functools.partial(_ffn_kernel, final_norm),
        grid=(m // tm,),
        in_specs=[pl.BlockSpec((tm, D_MODEL), lambda i: (i, 0)),
                  _resident((1, D_MODEL)),
                  _resident((D_MODEL, 2 * D_FF)), _resident((D_FF, D_MODEL)),
                  _resident((1, D_MODEL))],
        out_specs=pl.BlockSpec((tm, D_MODEL), lambda i: (i, 0)),
        out_shape=jax.ShapeDtypeStruct((m, D_MODEL), F32),
        scratch_shapes=[pltpu.VMEM((tm, D_MODEL), F32)],
        compiler_params=_params(1),
        name="ffn_final" if final_norm else "ffn",
    )(x2d, g, w_gate_up, w_down, g_final)


def _t5_bucket(rel):
    nb = NUM_BUCKETS // 2
    max_exact = nb // 2
    ret = jnp.where(rel > 0, nb, 0)
    n = jnp.abs(rel)
    nf = jnp.maximum(n, 1).astype(F32)
    large = max_exact + (jnp.log(nf / max_exact) / math.log(MAX_DISTANCE / max_exact)
                         * (nb - max_exact)).astype(jnp.int32)
    large = jnp.minimum(large, nb - 1)
    return ret + jnp.where(n < max_exact, n, large)


def _attention_bias(rel_bias):
    rel = jnp.arange(SPAN)[None, :] - WINDOW - jnp.arange(BLOCK)[:, None]
    bias = jnp.transpose(rel_bias[_t5_bucket(rel)], (2, 0, 1)).astype(F32)
    bias = jnp.where((jnp.abs(rel) <= WINDOW)[None], bias, NEG_BIG)
    col = jnp.arange(SPAN)
    variants = []
    for first, last in ((False, False), (True, False), (False, True), (True, True)):
        pad = (first & (col < WINDOW)) | (last & (col >= WINDOW + BLOCK))
        masked = jnp.where(pad[None, None, :], NEG_BIG, bias).reshape(N_KV_HEADS, Q_PER_KV, BLOCK, SPAN)
        variants.append(jnp.transpose(masked, (0, 3, 1, 2)).reshape(N_KV_HEADS * SPAN, Q_PER_KV * BLOCK))
    return jnp.stack(variants)


def _head_expand_matrix(lane0):
    e = np.zeros((2 * DT_PAD, D_INNER), np.float32)
    for h in range(SSD_HEADS):
        e[lane0 + h, h * SSD_HEAD_DIM:(h + 1) * SSD_HEAD_DIM] = 1.0
        e[DT_PAD + lane0 + h, h * SSD_HEAD_DIM:(h + 1) * SSD_HEAD_DIM] = 1.0
    return jnp.asarray(e, BF16)


def _tables(rel_bias):
    return {"attn_bias": _attention_bias(rel_bias), "expand_fwd": _head_expand_matrix(0),
            "expand_bwd": _head_expand_matrix(SSD_HEADS)}


def _prepare_layer(l, p):
    w_in = p["w_in"][l]
    o_q, o_k, o_v = 0, ATTN_WIDTH, ATTN_WIDTH + KV_WIDTH
    o_z = ATTN_WIDTH + 2 * KV_WIDTH
    o_xs = o_z + D_INNER
    o_bc = o_xs + D_INNER
    o_dt = o_bc + BC_WIDTH
    o_g = o_dt + 2 * SSD_HEADS
    w_main = jnp.concatenate([w_in[:, o_z:o_xs], w_in[:, o_xs:o_bc], w_in[:, o_g:o_g + 2 * D_MODEL],
                              w_in[:, o_q:o_k], w_in[:, o_bc:o_dt], w_in[:, o_k:o_v],
                              w_in[:, o_v:o_z]], axis=1).astype(BF16)
    w_dt = jnp.pad(w_in[:, o_dt:o_g], ((0, 0), (0, DT_PAD - 2 * SSD_HEADS))).astype(BF16)
    pad_dt = lambda a: jnp.pad(a.reshape(1, 2 * SSD_HEADS), ((0, 0), (0, DT_PAD - 2 * SSD_HEADS)))
    row = lambda a: a.reshape(1, -1).astype(F32)
    return {
        "norm_mix": row(p["norm_mix"][l]), "w_main": w_main, "w_dt": w_dt,
        "attn_sink": jnp.repeat(p["attn_sink"][l].astype(F32), BLOCK).reshape(1, N_HEADS * BLOCK),
        "conv_w": p["conv_w"][l].astype(F32), "conv_b": row(p["conv_b"][l]),
        "dt_bias": pad_dt(p["dt_bias"][l]), "a_log": pad_dt(p["a_log"][l]),
        "d_skip": row(jnp.repeat(p["d_skip"][l], SSD_HEAD_DIM)),
        "norm_ssd": row(p["norm_ssd"][l]),
        "w_proj_attn": p["w_proj_attn"][l].astype(BF16), "w_proj_ssd": p["w_proj_ssd"][l].astype(BF16),
        "w_out": p["w_out"][l].astype(BF16), "norm_cross": row(p["norm_cross"][l]),
        "norm_mem": row(p["norm_mem"][l]), "w_q_cross": p["w_q_cross"][l].astype(BF16),
        "w_kv_cross": p["w_kv_cross"][l].astype(BF16), "w_o_cross": p["w_o_cross"][l].astype(BF16),
        "norm_ffn": row(p["norm_ffn"][l]), "w_gate_up": p["w_gate_up"][l].astype(BF16),
        "w_down": p["w_down"][l].astype(BF16),
    }


def _encoder(x, mem, layers, tables, norm_final):
    b, s, _ = x.shape
    m = b * s
    tm = min(ROW_TILE, s)
    x2d = x.reshape(m, D_MODEL)
    for l, w in enumerate(layers):
        proj, dt = _in_proj(x2d, w, tm, s)
        proj3d = proj.reshape(b, s, PROJ_WIDTH)
        attn_o = _attention_call(proj3d, tables, w)
        ssd_o = _ssd_call(proj3d, dt.reshape(b, s, DT_PAD), tables, w)
        kv = _mem_kv(mem, w["norm_mem"], w["w_kv_cross"])
        x2d = _merge_cross(ssd_o.reshape(m, D_INNER), attn_o.reshape(m, ATTN_WIDTH), proj, x2d, kv, w,
                           tm, s)
        x2d = _ffn(x2d, w["norm_ffn"], w["w_gate_up"], w["w_down"], norm_final,
                   l == len(layers) - 1, tm)
    return x2d.reshape(b, s, D_MODEL)


def kernel(x_prompt, x_sample, mem_prompt, mem_sample, rel_bias, norm_mix, w_in, attn_sink, conv_w,
           conv_b, dt_bias, a_log, d_skip, norm_ssd, w_proj_attn, w_proj_ssd, w_out, norm_cross,
           norm_mem, w_q_cross, w_kv_cross, w_o_cross, norm_ffn, w_gate_up, w_down, norm_final):
    p = dict(norm_mix=norm_mix, w_in=w_in, attn_sink=attn_sink, conv_w=conv_w, conv_b=conv_b,
             dt_bias=dt_bias, a_log=a_log, d_skip=d_skip, norm_ssd=norm_ssd, w_proj_attn=w_proj_attn,
             w_proj_ssd=w_proj_ssd, w_out=w_out, norm_cross=norm_cross, norm_mem=norm_mem,
             w_q_cross=w_q_cross, w_kv_cross=w_kv_cross, w_o_cross=w_o_cross, norm_ffn=norm_ffn,
             w_gate_up=w_gate_up, w_down=w_down)
    layers = [_prepare_layer(l, p) for l in range(norm_mix.shape[0])]
    tables = _tables(rel_bias)
    g_final = norm_final.reshape(1, D_MODEL).astype(F32)
    y_prompt = _encoder(x_prompt, mem_prompt, layers, tables, g_final)
    y_sample = _encoder(x_sample, mem_sample, layers, tables, g_final)
    return (y_prompt, y_sample)
```

```python
import functools
import math

import numpy as np
import jax
import jax.numpy as jnp
from jax import lax
from jax.experimental import pallas as pl
from jax.experimental.pallas import tpu as pltpu

F32 = jnp.float32
BF16 = jnp.bfloat16

D_MODEL = 1024
N_HEADS = 16
N_KV_HEADS = 4
HEAD_DIM = 64
Q_PER_KV = N_HEADS // N_KV_HEADS
ATTN_WIDTH = N_HEADS * HEAD_DIM
KV_WIDTH = N_KV_HEADS * HEAD_DIM
WINDOW = 128
BLOCK = 128
SPAN = BLOCK + 2 * WINDOW
NUM_BUCKETS = 32
MAX_DISTANCE = 128
D_INNER = 2 * D_MODEL
SSD_HEAD_DIM = 64
SSD_HEADS = D_INNER // SSD_HEAD_DIM
SSD_GROUPS = 4
HEADS_PER_GROUP = SSD_HEADS // SSD_GROUPS
GROUP_WIDTH = HEADS_PER_GROUP * SSD_HEAD_DIM
D_STATE = 128
BC_WIDTH = 2 * SSD_GROUPS * D_STATE
CONV_K = 5
CONV_PAD = CONV_K // 2
CHUNK = 128
MEM_LEN = 256
CROSS_HEADS = 4
CROSS_HEAD_DIM = D_MODEL // CROSS_HEADS
D_FF = ((8 * D_MODEL // 3 + 255) // 256) * 256
EPS = 1e-6
LOG2_E = math.log2(math.e)

PROJ_WIDTH = 2 * D_INNER + 2 * D_MODEL + ATTN_WIDTH + BC_WIDTH + 2 * KV_WIDTH
COL_Z = 0
COL_XS = 1
COL_GATES = 2
COL_Q = (3 * D_INNER) // ATTN_WIDTH
COL_BC = (3 * D_INNER + ATTN_WIDTH) // BC_WIDTH
COL_K = (3 * D_INNER + ATTN_WIDTH + BC_WIDTH) // KV_WIDTH
COL_V = COL_K + 1
DT_PAD = 128

LANES = 128
SUBLANES = 8
HALO_ROWS = 16
CONV_ROWS = 128
SSD_CHUNKS_PER_STEP = 4
ATTN_BLOCKS_PER_STEP = 4
MATMUL_N_CHUNK = 512
ROW_TILE = 512
NEG_BIG = -1e30
VMEM_LIMIT = 56 * 1024 * 1024


def _rms(x, g):
    return x * lax.rsqrt(jnp.mean(x * x, axis=-1, keepdims=True) + EPS) * g


def _silu(x):
    return x * (1.0 / (1.0 + jnp.exp(-x)))


def _sigmoid(x):
    return 1.0 / (1.0 + jnp.exp(-x))


def _softplus(x):
    return jnp.maximum(x, 0.0) + jnp.log1p(jnp.exp(-jnp.abs(x)))


def _dot(a, b):
    return jnp.dot(a, b, preferred_element_type=F32)


def _dot_nt(a, b):
    return lax.dot_general(a, b, (((1,), (1,)), ((), ())), preferred_element_type=F32)


def _resident(shape):
    nd = len(shape)
    return pl.BlockSpec(shape, lambda *_: (0,) * nd, pipeline_mode=pl.Buffered(1))


def _params(n_axes):
    return pltpu.CompilerParams(dimension_semantics=("arbitrary",) * n_axes,
                                vmem_limit_bytes=VMEM_LIMIT)


def _in_proj_kernel(tiles_per_seq, x_ref, xp_ref, xn_ref, g_ref, w_ref, wdt_ref, cw_ref, cb_ref,
                    o_ref, dt_ref, ext):
    i = pl.program_id(0)
    tm = x_ref.shape[0]
    g = g_ref[...]
    h = _rms(x_ref[...], g).astype(BF16)
    hp = jnp.where(i % tiles_per_seq != 0, _rms(xp_ref[...], g), 0.0)
    hn = jnp.where((i + 1) % tiles_per_seq != 0, _rms(xn_ref[...], g), 0.0)
    h_ext = jnp.concatenate([hp.astype(BF16), h, hn.astype(BF16)], axis=0)
    blk_rows = CONV_ROWS + 2 * SUBLANES

    def finish(n0, main):
        conv_col = _conv_column(n0)
        if conv_col is None:
            if n0 < (COL_Z + 1) * D_INNER:
                main = _silu(main)
            o_ref[:, n0:n0 + MATMUL_N_CHUNK] = main.astype(BF16)
            return
        ext[...] = main
        for r0 in range(0, tm, CONV_ROWS):
            for l0 in range(0, MATMUL_N_CHUNK, LANES):
                cc = slice(conv_col + l0, conv_col + l0 + LANES)
                blk = ext[HALO_ROWS - SUBLANES + r0:HALO_ROWS - SUBLANES + r0 + blk_rows, l0:l0 + LANES]
                acc = cb_ref[:, cc] + cw_ref[CONV_PAD:CONV_PAD + 1, cc] * blk[SUBLANES:SUBLANES + CONV_ROWS]
                for t in range(CONV_K):
                    if t != CONV_PAD:
                        shifted = pltpu.roll(blk, (CONV_PAD - t) % blk_rows, axis=0)
                        acc = acc + cw_ref[t:t + 1, cc] * shifted[SUBLANES:SUBLANES + CONV_ROWS]
                o_ref[r0:r0 + CONV_ROWS, n0 + l0:n0 + l0 + LANES] = _silu(acc).astype(BF16)

    pending = None
    for n0 in _chunk_order():
        cols = slice(n0, n0 + MATMUL_N_CHUNK)
        main = _dot(h if _conv_column(n0) is None else h_ext, w_ref[:, cols])
        if pending is not None:
            finish(*pending)
        pending = (n0, main)
    dt_ref[...] = _dot(h, wdt_ref[...])
    finish(*pending)


def _chunk_order():
    chunks = list(range(0, PROJ_WIDTH, MATMUL_N_CHUNK))
    heavy = [n0 for n0 in chunks if _conv_column(n0) is not None]
    light = [n0 for n0 in chunks if _conv_column(n0) is None]
    order = []
    while heavy or light:
        if heavy:
            order.append(heavy.pop(0))
        take = -(-len(light) // (len(heavy) + 1))
        order.extend(light[:take])
        light = light[take:]
    return order


def _conv_column(n0):
    if COL_XS * D_INNER <= n0 < (COL_XS + 1) * D_INNER:
        return n0 - COL_XS * D_INNER
    if COL_BC * BC_WIDTH <= n0 < (COL_BC + 1) * BC_WIDTH:
        return D_INNER + n0 - COL_BC * BC_WIDTH
    return None


def _in_proj(x2d, w, tm, seq):
    m = x2d.shape[0]
    halo_per_tile = tm // HALO_ROWS
    n_halo = m // HALO_ROWS
    return pl.pallas_call(
        functools.partial(_in_proj_kernel, seq // tm),
        grid=(m // tm,),
        in_specs=[pl.BlockSpec((tm, D_MODEL), lambda i: (i, 0)),
                  pl.BlockSpec((HALO_ROWS, D_MODEL), lambda i: (jnp.maximum(i * halo_per_tile - 1, 0), 0)),
                  pl.BlockSpec((HALO_ROWS, D_MODEL),
                               lambda i: (jnp.minimum((i + 1) * halo_per_tile, n_halo - 1), 0)),
                  _resident((1, D_MODEL)),
                  _resident((D_MODEL, PROJ_WIDTH)),
                  _resident((D_MODEL, DT_PAD)),
                  _resident((CONV_K, D_INNER + BC_WIDTH)), _resident((1, D_INNER + BC_WIDTH))],
        out_specs=[pl.BlockSpec((tm, PROJ_WIDTH), lambda i: (i, 0)),
                   pl.BlockSpec((tm, DT_PAD), lambda i: (i, 0))],
        out_shape=[jax.ShapeDtypeStruct((m, PROJ_WIDTH), BF16),
                   jax.ShapeDtypeStruct((m, DT_PAD), F32)],
        scratch_shapes=[pltpu.VMEM((tm + 2 * HALO_ROWS, MATMUL_N_CHUNK), F32)],
        compiler_params=_params(1),
        name="in_proj",
    )(x2d, x2d, x2d, w["norm_mix"], w["w_main"], w["w_dt"], w["conv_w"], w["conv_b"])


def _attn_kernel(n_sub, q_ref, kp_ref, kc_ref, kn_ref, vp_ref, vc_ref, vn_ref, sink_ref, *refs):
    bias_refs, o_ref = refs[:n_sub], refs[n_sub]
    k_all = jnp.concatenate([kp_ref[0], kc_ref[0], kn_ref[0]], axis=0)
    v_all = jnp.concatenate([vp_ref[0], vc_ref[0], vn_ref[0]], axis=0)
    vt_all = v_all.astype(F32).T.astype(BF16)
    cols = Q_PER_KV * BLOCK
    ones_rows = (lax.broadcasted_iota(jnp.int32, (HALO_ROWS, SPAN), 0) == 0).astype(BF16)
    for sub in range(n_sub):
        q = q_ref[0, sub * BLOCK:(sub + 1) * BLOCK, :] * jnp.asarray(HEAD_DIM ** -0.5, BF16)
        k = k_all[sub * BLOCK:sub * BLOCK + SPAN]
        v_t = vt_all[:, sub * BLOCK:sub * BLOCK + SPAN]
        bias_ref = bias_refs[sub]

        def logits(g):
            q4 = jnp.concatenate([q[:, (g * Q_PER_KV + r) * HEAD_DIM:(g * Q_PER_KV + r + 1) * HEAD_DIM]
                                  for r in range(Q_PER_KV)], axis=0)
            return (_dot_nt(k[:, g * HEAD_DIM:(g + 1) * HEAD_DIM], q4)
                    + bias_ref[0, g * SPAN:(g + 1) * SPAN, :])

        outs = []
        s_next = logits(0)
        for g in range(N_KV_HEADS):
            s = s_next
            if g + 1 < N_KV_HEADS:
                s_next = logits(g + 1)
            sink = sink_ref[:, g * cols:(g + 1) * cols]
            m = jnp.maximum(jnp.max(s, axis=0, keepdims=True), sink)
            e = jnp.exp(s - m).astype(BF16)
            v_aug = jnp.concatenate([v_t[g * HEAD_DIM:(g + 1) * HEAD_DIM], ones_rows], axis=0)
            o_aug = _dot(v_aug, e)
            den = o_aug[HEAD_DIM:HEAD_DIM + 1] + jnp.exp(sink - m)
            o_t = o_aug[:HEAD_DIM] * (1.0 / den)
            outs.extend(o_t[:, r * BLOCK:(r + 1) * BLOCK] for r in range(Q_PER_KV))
        o_ref[0, sub * BLOCK:(sub + 1) * BLOCK, :] = jnp.concatenate(outs, axis=0).T.astype(BF16)


def _window_attention(proj3d, bias, sink):
    b, s, _ = proj3d.shape
    nblk = s // BLOCK
    n_sub = ATTN_BLOCKS_PER_STEP if nblk % ATTN_BLOCKS_PER_STEP == 0 else 1
    nstep = nblk // n_sub

    def edge_spec(col, index):
        return pl.BlockSpec((1, BLOCK, KV_WIDTH), lambda bi, i: (bi, index(i), col))

    def kv_specs(col):
        return [edge_spec(col, lambda i: jnp.maximum(i * n_sub - 1, 0)),
                pl.BlockSpec((1, n_sub * BLOCK, KV_WIDTH), lambda bi, i: (bi, i, col)),
                edge_spec(col, lambda i: jnp.minimum((i + 1) * n_sub, nblk - 1))]

    def bias_spec(sub):
        def variant(bi, i):
            first = (i * n_sub + sub == 0).astype(jnp.int32)
            last = (i * n_sub + sub == nblk - 1).astype(jnp.int32)
            return (first + 2 * last, 0, 0)
        return pl.BlockSpec((1, N_KV_HEADS * SPAN, Q_PER_KV * BLOCK), variant)

    return pl.pallas_call(
        functools.partial(_attn_kernel, n_sub),
        grid=(b, nstep),
        in_specs=[pl.BlockSpec((1, n_sub * BLOCK, ATTN_WIDTH), lambda bi, i: (bi, i, COL_Q)),
                  *kv_specs(COL_K), *kv_specs(COL_V), _resident((1, N_HEADS * BLOCK)),
                  *[bias_spec(sub) for sub in range(n_sub)]],
        out_specs=pl.BlockSpec((1, n_sub * BLOCK, ATTN_WIDTH), lambda bi, i: (bi, i, 0)),
        out_shape=jax.ShapeDtypeStruct((b, s, ATTN_WIDTH), BF16),
        compiler_params=_params(2),
        name="window_attention",
    )(proj3d, proj3d, proj3d, proj3d, proj3d, proj3d, proj3d, sink, *([bias] * n_sub))


def _attention_call(proj3d, tables, w):
    return _window_attention(proj3d, tables["attn_bias"], w["attn_sink"])


def _split_hi_lo(v):
    hi = v.astype(BF16)
    return hi, (v - hi.astype(F32)).astype(BF16)


def _scan_tables(forward, dt_raw, dtb_ref, alog_ref, e2_ref):
    total_row = CHUNK - 1 if forward else 0
    dtv = _softplus(dt_raw + dtb_ref[...])
    adt = dtv * (-jnp.exp(alog_ref[...])) * LOG2_E
    ii = lax.broadcasted_iota(jnp.int32, (CHUNK, CHUNK), 0)
    jj = lax.broadcasted_iota(jnp.int32, (CHUNK, CHUNK), 1)
    scanned = (jj <= ii) if forward else (jj >= ii)
    tri = scanned.astype(BF16)
    a1 = adt.astype(BF16)
    r1 = adt - a1.astype(F32)
    a2 = r1.astype(BF16)
    a3 = (r1 - a2.astype(F32)).astype(BF16)
    pc = _dot(jnp.concatenate([tri, tri, tri], axis=1), jnp.concatenate([a1, a2, a3], axis=0))
    total = pc[total_row:total_row + 1, :]
    both = jnp.concatenate([jnp.exp2(pc), jnp.exp2(total - pc) * dtv], axis=0)
    hi, lo = _split_hi_lo(both)
    both_e = _dot(jnp.concatenate([hi, lo], axis=1), e2_ref[...])
    off_e = both_e[:CHUNK]
    w_e = both_e[CHUNK:]
    decay = off_e[total_row:total_row + 1, :]
    row_t = (pc - jnp.log(dtv) * LOG2_E).T
    return scanned, pc, row_t, off_e, w_e, decay


def _scan_group(g, lane0, scanned, pc, row_t, off_e, decay, b_t, c_bf, x_bf, xw_bf, state):
    gcols = slice(g * GROUP_WIDTH, (g + 1) * GROUP_WIDTH)
    gmat = _dot(c_bf, b_t)
    sg = state[g]
    y_off = _dot(c_bf, sg.astype(BF16)) * off_e[:, gcols]

    def decay_block(r):
        lane = lane0 + g * HEADS_PER_GROUP + r
        delta = pc[:, lane:lane + 1] - row_t[lane:lane + 1, :]
        return (gmat * jnp.exp2(jnp.where(scanned, delta, NEG_BIG))).astype(BF16)

    first_head = lax.broadcasted_iota(jnp.int32, (1, LANES), 1) < SSD_HEAD_DIM
    ys = []
    for q in range(HEADS_PER_GROUP // 2):
        x_pair = x_bf[:, q * LANES:(q + 1) * LANES]
        zero = jnp.zeros_like(x_pair)
        rhs = jnp.concatenate([jnp.where(first_head, x_pair, zero), jnp.where(first_head, zero, x_pair)],
                              axis=0)
        ys.append(_dot(jnp.concatenate([decay_block(2 * q), decay_block(2 * q + 1)], axis=1), rhs))
    state[g] = sg * decay[:, gcols] + _dot(b_t, xw_bf)
    return jnp.concatenate(ys, axis=-1) + y_off


def _ssd_bwd_kernel(x_ref, bc_ref, dt_ref, dtb_ref, alog_ref, e2_ref, y_ref, bt_ref, state):
    @pl.when(pl.program_id(1) == 0)
    def _():
        state[...] = jnp.zeros_like(state)

    order = list(reversed(range(x_ref.shape[1] // CHUNK)))
    tabs = {ci: _scan_tables(False, dt_ref[0, ci * CHUNK:(ci + 1) * CHUNK, :], dtb_ref, alog_ref, e2_ref)
            for ci in order}
    bc_half = BC_WIDTH // 2
    for ci in order:
        rows = slice(ci * CHUNK, (ci + 1) * CHUNK)
        scanned, pc, row_t, off_e, w_e, decay = tabs[ci]
        for g in range(SSD_GROUPS):
            gcols = slice(g * GROUP_WIDTH, (g + 1) * GROUP_WIDTH)
            x_bf = x_ref[0, rows, gcols]
            b_t = bc_ref[0, rows, g * D_STATE:(g + 1) * D_STATE].astype(F32).T.astype(BF16)
            bt_ref[0, ci * bc_half + g * D_STATE:ci * bc_half + (g + 1) * D_STATE, :] = b_t
            c_bf = bc_ref[0, rows, (SSD_GROUPS + g) * D_STATE:(SSD_GROUPS + g + 1) * D_STATE]
            y = _scan_group(g, SSD_HEADS, scanned, pc, row_t, off_e, decay, b_t, c_bf, x_bf,
                            (x_bf.astype(F32) * w_e[:, gcols]).astype(BF16), state)
            y_ref[0, rows, gcols] = y.astype(BF16)


def _ssd_fwd_kernel(x_ref, c_ref, bt_ref, dt_ref, dtb_ref, alog_ref, e2_ref, yb_ref, z_ref, dskip_ref,
                    gn_ref, out_ref, state, ybuf):
    @pl.when(pl.program_id(1) == 0)
    def _():
        state[...] = jnp.zeros_like(state)

    order = list(range(x_ref.shape[1] // CHUNK))
    tabs = {ci: _scan_tables(True, dt_ref[0, ci * CHUNK:(ci + 1) * CHUNK, :], dtb_ref, alog_ref, e2_ref)
            for ci in order}
    bc_half = BC_WIDTH // 2
    for ci in order:
        rows = slice(ci * CHUNK, (ci + 1) * CHUNK)
        scanned, pc, row_t, off_e, w_e, decay = tabs[ci]
        for g in range(SSD_GROUPS):
            gcols = slice(g * GROUP_WIDTH, (g + 1) * GROUP_WIDTH)
            x_bf = x_ref[0, rows, gcols]
            xg = x_bf.astype(F32)
            b_t = bt_ref[0, ci * bc_half + g * D_STATE:ci * bc_half + (g + 1) * D_STATE, :]
            c_bf = c_ref[0, rows, g * D_STATE:(g + 1) * D_STATE]
            y = _scan_group(g, 0, scanned, pc, row_t, off_e, decay, b_t, c_bf, x_bf,
                            (xg * w_e[:, gcols]).astype(BF16), state)
            ybuf[rows, gcols] = y + yb_ref[0, rows, gcols].astype(F32) + xg * dskip_ref[:, gcols]
        gated = ybuf[rows, :] * z_ref[0, rows, :].astype(F32)
        out_ref[0, rows, :] = _rms(gated, gn_ref[...]).astype(BF16)


def _ssd_call(proj3d, dt3d, tables, w):
    b, s, _ = proj3d.shape
    nchunk = s // CHUNK
    per_step = SSD_CHUNKS_PER_STEP if nchunk % SSD_CHUNKS_PER_STEP == 0 else 1
    nstep = nchunk // per_step
    rows = per_step * CHUNK
    bc_half = BC_WIDTH // 2

    def rev_spec(width, col=0):
        return pl.BlockSpec((1, rows, width), lambda bi, c: (bi, nstep - 1 - c, col))

    def chunk_spec(width, col=0):
        return pl.BlockSpec((1, rows, width), lambda bi, c: (bi, c, col))

    state = pltpu.VMEM((SSD_GROUPS, D_STATE, GROUP_WIDTH), F32)
    y_bwd, bt = pl.pallas_call(
        _ssd_bwd_kernel,
        grid=(b, nstep),
        in_specs=[rev_spec(D_INNER, COL_XS), rev_spec(BC_WIDTH, COL_BC), rev_spec(DT_PAD),
                  _resident((1, DT_PAD)), _resident((1, DT_PAD)), _resident((2 * DT_PAD, D_INNER))],
        out_specs=[rev_spec(D_INNER),
                   pl.BlockSpec((1, per_step * bc_half, CHUNK), lambda bi, c: (bi, nstep - 1 - c, 0))],
        out_shape=[jax.ShapeDtypeStruct((b, s, D_INNER), BF16),
                   jax.ShapeDtypeStruct((b, nchunk * bc_half, CHUNK), BF16)],
        scratch_shapes=[state],
        compiler_params=_params(2),
        name="ssd_backward",
    )(proj3d, proj3d, dt3d, w["dt_bias"], w["a_log"], tables["expand_bwd"])

    return pl.pallas_call(
        _ssd_fwd_kernel,
        grid=(b, nstep),
        in_specs=[chunk_spec(D_INNER, COL_XS), chunk_spec(bc_half, COL_BC * 2 + 1),
                  pl.BlockSpec((1, per_step * bc_half, CHUNK), lambda bi, c: (bi, c, 0)),
                  chunk_spec(DT_PAD),
                  _resident((1, DT_PAD)), _resident((1, DT_PAD)), _resident((2 * DT_PAD, D_INNER)),
                  chunk_spec(D_INNER), chunk_spec(D_INNER, COL_Z),
                  _resident((1, D_INNER)), _resident((1, D_INNER))],
        out_specs=chunk_spec(D_INNER),
        out_shape=jax.ShapeDtypeStruct((b, s, D_INNER), BF16),
        scratch_shapes=[state, pltpu.VMEM((rows, D_INNER), F32)],
        compiler_params=_params(2),
        name="ssd_forward",
    )(proj3d, proj3d, bt, dt3d, w["dt_bias"], w["a_log"], tables["expand_fwd"], y_bwd, proj3d,
      w["d_skip"], w["norm_ssd"])


def _mem_kv_kernel(mem_ref, g_ref, w_ref, o_ref):
    h = _rms(mem_ref[0], g_ref[...]).astype(BF16)
    o_ref[0] = _dot(h, w_ref[...]).astype(BF16)


def _mem_kv(mem, g, w_kv):
    b = mem.shape[0]
    return pl.pallas_call(
        _mem_kv_kernel,
        grid=(b,),
        in_specs=[pl.BlockSpec((1, MEM_LEN, D_MODEL), lambda i: (i, 0, 0)),
                  _resident((1, D_MODEL)), _resident((D_MODEL, 2 * D_MODEL))],
        out_specs=pl.BlockSpec((1, MEM_LEN, 2 * D_MODEL), lambda i: (i, 0, 0)),
        out_shape=jax.ShapeDtypeStruct((b, MEM_LEN, 2 * D_MODEL), BF16),
        compiler_params=_params(1),
        name="memory_kv",
    )(mem, g, w_kv)


def _merge_cross_kernel(ssd_ref, att_ref, gate_ref, x_ref, kv_ref, wpa_ref, wps_ref, wout_ref,
                        gc_ref, wq_ref, wo_ref, o_ref):
    gates = _sigmoid(gate_ref[...].astype(F32))
    mixed = (gates[:, :D_MODEL] * _dot(att_ref[...], wpa_ref[...])
             + gates[:, D_MODEL:] * _dot(ssd_ref[...], wps_ref[...]))
    x1 = x_ref[...] + _dot(mixed.astype(BF16), wout_ref[...])
    h = _rms(x1, gc_ref[...]).astype(BF16)
    q = (_dot(h, wq_ref[...]) * CROSS_HEAD_DIM ** -0.5).astype(BF16)
    outs = []
    for hh in range(CROSS_HEADS):
        hs = slice(hh * CROSS_HEAD_DIM, (hh + 1) * CROSS_HEAD_DIM)
        vs = slice(D_MODEL + hh * CROSS_HEAD_DIM, D_MODEL + (hh + 1) * CROSS_HEAD_DIM)
        s = _dot_nt(q[:, hs], kv_ref[0, :, hs])
        e = jnp.exp(s - jnp.max(s, axis=-1, keepdims=True))
        den = jnp.sum(e, axis=-1, keepdims=True)
        outs.append(_dot(e.astype(BF16), kv_ref[0, :, vs]) * (1.0 / den))
    o = jnp.concatenate(outs, axis=-1).astype(BF16)
    o_ref[...] = x1 + _dot(o, wo_ref[...])


def _merge_cross(ssd_o, attn_o, proj2d, x2d, kv, w, tm, seq):
    m = x2d.shape[0]
    tiles_per_seq = seq // tm
    return pl.pallas_call(
        _merge_cross_kernel,
        grid=(m // tm,),
        in_specs=[pl.BlockSpec((tm, D_INNER), lambda i: (i, 0)),
                  pl.BlockSpec((tm, ATTN_WIDTH), lambda i: (i, 0)),
                  pl.BlockSpec((tm, 2 * D_MODEL), lambda i: (i, COL_GATES)),
                  pl.BlockSpec((tm, D_MODEL), lambda i: (i, 0)),
                  pl.BlockSpec((1, MEM_LEN, 2 * D_MODEL), lambda i: (i // tiles_per_seq, 0, 0)),
                  _resident((ATTN_WIDTH, D_MODEL)), _resident((D_INNER, D_MODEL)),
                  _resident((D_MODEL, D_MODEL)), _resident((1, D_MODEL)),
                  _resident((D_MODEL, D_MODEL)), _resident((D_MODEL, D_MODEL))],
        out_specs=pl.BlockSpec((tm, D_MODEL), lambda i: (i, 0)),
        out_shape=jax.ShapeDtypeStruct((m, D_MODEL), F32),
        compiler_params=_params(1),
        name="merge_cross",
    )(ssd_o, attn_o, proj2d, x2d, kv, w["w_proj_attn"], w["w_proj_ssd"], w["w_out"],
      w["norm_cross"], w["w_q_cross"], w["w_o_cross"])


def _ffn_chunks():
    chunks, f0 = [], 0
    while f0 < D_FF:
        fw = min(MATMUL_N_CHUNK, D_FF - f0)
        chunks.append((f0, fw))
        f0 += fw
    return chunks


def _ffn_kernel(final_norm, x_ref, g_ref, wgu_ref, wd_ref, gf_ref, o_ref, acc_ref):
    x = x_ref[...]
    h = _rms(x, g_ref[...]).astype(BF16)
    for n, (f0, fw) in enumerate(_ffn_chunks()):
        gate = _dot(h, wgu_ref[:, f0:f0 + fw])
        up = _dot(h, wgu_ref[:, D_FF + f0:D_FF + f0 + fw])
        part = _dot((_silu(gate) * up).astype(BF16), wd_ref[f0:f0 + fw, :])
        if n == 0:
            acc_ref[...] = x + part
        else:
            acc_ref[...] += part
    y = acc_ref[...]
    o_ref[...] = _rms(y, gf_ref[...]) if final_norm else y


def _ffn(x2d, g, w_gate_up, w_down, g_final, final_norm, tm):
    m = x2d.shape[0]
    return pl.pallas_call(
        functools.partial(_ffn_kernel, final_norm),
        grid=(m // tm,),
        in_specs=[pl.BlockSpec((tm, D_MODEL), lambda i: (i, 0)),
                  _resident((1, D_MODEL)),
                  _resident((D_MODEL, 2 * D_FF)), _resident((D_FF, D_MODEL)),
                  _resident((1, D_MODEL))],
        out_specs=pl.BlockSpec((tm, D_MODEL), lambda i: (i, 0)),
        out_shape=jax.ShapeDtypeStruct((m, D_MODEL), F32),
        scratch_shapes=[pltpu.VMEM((tm, D_MODEL), F32)],
        compiler_params=_params(1),
        name="ffn_final" if final_norm else "ffn",
    )(x2d, g, w_gate_up, w_down, g_final)


def _t5_bucket(rel):
    nb = NUM_BUCKETS // 2
    max_exact = nb // 2
    ret = jnp.where(rel > 0, nb, 0)
    n = jnp.abs(rel)
    nf = jnp.maximum(n, 1).astype(F32)
    large = max_exact + (jnp.log(nf / max_exact) / math.log(MAX_DISTANCE / max_exact)
                         * (nb - max_exact)).astype(jnp.int32)
    large = jnp.minimum(large, nb - 1)
    return ret + jnp.where(n < max_exact, n, large)


def _attention_bias(rel_bias):
    rel = jnp.arange(SPAN)[None, :] - WINDOW - jnp.arange(BLOCK)[:, None]
    bias = jnp.transpose(rel_bias[_t5_bucket(rel)], (2, 0, 1)).astype(F32)
    bias = jnp.where((jnp.abs(rel) <= WINDOW)[None], bias, NEG_BIG)
    col = jnp.arange(SPAN)
    variants = []
    for first, last in ((False, False), (True, False), (False, True), (True, True)):
        pad = (first & (col < WINDOW)) | (last & (col >= WINDOW + BLOCK))
        masked = jnp.where(pad[None, None, :], NEG_BIG, bias).reshape(N_KV_HEADS, Q_PER_KV, BLOCK, SPAN)
        variants.append(jnp.transpose(masked, (0, 3, 1, 2)).reshape(N_KV_HEADS * SPAN, Q_PER_KV * BLOCK))
    return jnp.stack(variants)


def _head_expand_matrix(lane0):
    e = np.zeros((2 * DT_PAD, D_INNER), np.float32)
    for h in range(SSD_HEADS):
        e[lane0 + h, h * SSD_HEAD_DIM:(h + 1) * SSD_HEAD_DIM] = 1.0
        e[DT_PAD + lane0 + h, h * SSD_HEAD_DIM:(h + 1) * SSD_HEAD_DIM] = 1.0
    return jnp.asarray(e, BF16)


def _tables(rel_bias):
    return {"attn_bias": _attention_bias(rel_bias), "expand_fwd": _head_expand_matrix(0),
            "expand_bwd": _head_expand_matrix(SSD_HEADS)}


def _prepare_layer(l, p):
    w_in = p["w_in"][l]
    o_q, o_k, o_v = 0, ATTN_WIDTH, ATTN_WIDTH + KV_WIDTH
    o_z = ATTN_WIDTH + 2 * KV_WIDTH
    o_xs = o_z + D_INNER
    o_bc = o_xs + D_INNER
    o_dt = o_bc + BC_WIDTH
    o_g = o_dt + 2 * SSD_HEADS
    w_main = jnp.concatenate([w_in[:, o_z:o_xs], w_in[:, o_xs:o_bc], w_in[:, o_g:o_g + 2 * D_MODEL],
                              w_in[:, o_q:o_k], w_in[:, o_bc:o_dt], w_in[:, o_k:o_v],
                              w_in[:, o_v:o_z]], axis=1).astype(BF16)
    w_dt = jnp.pad(w_in[:, o_dt:o_g], ((0, 0), (0, DT_PAD - 2 * SSD_HEADS))).astype(BF16)
    pad_dt = lambda a: jnp.pad(a.reshape(1, 2 * SSD_HEADS), ((0, 0), (0, DT_PAD - 2 * SSD_HEADS)))
    row = lambda a: a.reshape(1, -1).astype(F32)
    return {
        "norm_mix": row(p["norm_mix"][l]), "w_main": w_main, "w_dt": w_dt,
        "attn_sink": jnp.repeat(p["attn_sink"][l].astype(F32), BLOCK).reshape(1, N_HEADS * BLOCK),
        "conv_w": p["conv_w"][l].astype(F32), "conv_b": row(p["conv_b"][l]),
        "dt_bias": pad_dt(p["dt_bias"][l]), "a_log": pad_dt(p["a_log"][l]),
        "d_skip": row(jnp.repeat(p["d_skip"][l], SSD_HEAD_DIM)),
        "norm_ssd": row(p["norm_ssd"][l]),
        "w_proj_attn": p["w_proj_attn"][l].astype(BF16), "w_proj_ssd": p["w_proj_ssd"][l].astype(BF16),
        "w_out": p["w_out"][l].astype(BF16), "norm_cross": row(p["norm_cross"][l]),
        "norm_mem": row(p["norm_mem"][l]), "w_q_cross": p["w_q_cross"][l].astype(BF16),
        "w_kv_cross": p["w_kv_cross"][l].astype(BF16), "w_o_cross": p["w_o_cross"][l].astype(BF16),
        "norm_ffn": row(p["norm_ffn"][l]), "w_gate_up": p["w_gate_up"][l].astype(BF16),
        "w_down": p["w_down"][l].astype(BF16),
    }


def _encoder(x, mem, layers, tables, norm_final):
    b, s, _ = x.shape
    m = b * s
    tm = min(ROW_TILE, s)
    x2d = x.reshape(m, D_MODEL)
    for l, w in enumerate(layers):
        proj, dt = _in_proj(x2d, w, tm, s)
        proj3d = proj.reshape(b, s, PROJ_WIDTH)
        attn_o = _attention_call(proj3d, tables, w)
        ssd_o = _ssd_call(proj3d, dt.reshape(b, s, DT_PAD), tables, w)
        kv = _mem_kv(mem, w["norm_mem"], w["w_kv_cross"])
        x2d = _merge_cross(ssd_o.reshape(m, D_INNER), attn_o.reshape(m, ATTN_WIDTH), proj, x2d, kv, w,
                           tm, s)
        x2d = _ffn(x2d, w["norm_ffn"], w["w_gate_up"], w["w_down"], norm_final,
                   l == len(layers) - 1, tm)
    return x2d.reshape(b, s, D_MODEL)


def kernel(x_prompt, x_sample, mem_prompt, mem_sample, rel_bias, norm_mix, w_in, attn_sink, conv_w,
           conv_b, dt_bias, a_log, d_skip, norm_ssd, w_proj_attn, w_proj_ssd, w_out, norm_cross,
           norm_mem, w_q_cross, w_kv_cross, w_o_cross, norm_ffn, w_gate_up, w_down, norm_final):
    p = dict(norm_mix=norm_mix, w_in=w_in, attn_sink=attn_sink, conv_w=conv_w, conv_b=conv_b,
             dt_bias=dt_bias, a_log=a_log, d_skip=d_skip, norm_ssd=norm_ssd, w_proj_attn=w_proj_attn,
             w_proj_ssd=w_proj_ssd, w_out=w_out, norm_cross=norm_cross, norm_mem=norm_mem,
             w_q_cross=w_q_cross, w_kv_cross=w_kv_cross, w_o_cross=w_o_cross, norm_ffn=norm_ffn,
             w_gate_up=w_gate_up, w_down=w_down)
    layers = [_prepare_layer(l, p) for l in range(norm_mix.shape[0])]
    tables = _tables(rel_bias)
    g_final = norm_final.reshape(1, D_MODEL).astype(F32)
    y_prompt = _encoder(x_prompt, mem_prompt, layers, tables, g_final)
    y_sample = _encoder(x_sample, mem_sample, layers, tables, g_final)
    return (y_prompt, y_sample)
```

```python
import functools
import math

import numpy as np
import jax
import jax.numpy as jnp
from jax import lax
from jax.experimental import pallas as pl
from jax.experimental.pallas import tpu as pltpu

F32 = jnp.float32
BF16 = jnp.bfloat16

D_MODEL = 1024
N_HEADS = 16
N_KV_HEADS = 4
HEAD_DIM = 64
Q_PER_KV = N_HEADS // N_KV_HEADS
ATTN_WIDTH = N_HEADS * HEAD_DIM
KV_WIDTH = N_KV_HEADS * HEAD_DIM
WINDOW = 128
BLOCK = 128
SPAN = BLOCK + 2 * WINDOW
NUM_BUCKETS = 32
MAX_DISTANCE = 128
D_INNER = 2 * D_MODEL
SSD_HEAD_DIM = 64
SSD_HEADS = D_INNER // SSD_HEAD_DIM
SSD_GROUPS = 4
HEADS_PER_GROUP = SSD_HEADS // SSD_GROUPS
GROUP_WIDTH = HEADS_PER_GROUP * SSD_HEAD_DIM
D_STATE = 128
BC_WIDTH = 2 * SSD_GROUPS * D_STATE
CONV_K = 5
CONV_PAD = CONV_K // 2
CHUNK = 128
MEM_LEN = 256
CROSS_HEADS = 4
CROSS_HEAD_DIM = D_MODEL // CROSS_HEADS
D_FF = ((8 * D_MODEL // 3 + 255) // 256) * 256
EPS = 1e-6
LOG2_E = math.log2(math.e)

PROJ_WIDTH = 2 * D_INNER + 2 * D_MODEL + ATTN_WIDTH + BC_WIDTH + 2 * KV_WIDTH
COL_Z = 0
COL_XS = 1
COL_GATES = 2
COL_Q = (3 * D_INNER) // ATTN_WIDTH
COL_BC = (3 * D_INNER + ATTN_WIDTH) // BC_WIDTH
COL_K = (3 * D_INNER + ATTN_WIDTH + BC_WIDTH) // KV_WIDTH
COL_V = COL_K + 1
DT_PAD = 128

LANES = 128
SUBLANES = 8
HALO_ROWS = 16
CONV_ROWS = 64
SSD_CHUNKS_PER_STEP = 4
ATTN_BLOCKS_PER_STEP = 4
MATMUL_N_CHUNK = 512
ROW_TILE = 512
NEG_BIG = -1e30
VMEM_LIMIT = 56 * 1024 * 1024


def _rms(x, g):
    return x * lax.rsqrt(jnp.mean(x * x, axis=-1, keepdims=True) + EPS) * g


def _silu(x):
    return x * lax.logistic(x)


def _sigmoid(x):
    return lax.logistic(x)


def _softplus(x):
    return jnp.maximum(x, 0.0) + jnp.log1p(jnp.exp(-jnp.abs(x)))


def _dot(a, b):
    return jnp.dot(a, b, preferred_element_type=F32)


def _dot_nt(a, b):
    return lax.dot_general(a, b, (((1,), (1,)), ((), ())), preferred_element_type=F32)


def _resident(shape):
    nd = len(shape)
    return pl.BlockSpec(shape, lambda *_: (0,) * nd, pipeline_mode=pl.Buffered(1))


def _params(n_axes):
    return pltpu.CompilerParams(dimension_semantics=("arbitrary",) * n_axes,
                                vmem_limit_bytes=VMEM_LIMIT)


def _in_proj_kernel(tiles_per_seq, x_ref, xp_ref, xn_ref, g_ref, w_ref, wdt_ref, cw_ref, cb_ref,
                    o_ref, dt_ref, h_ext, ext):
    i = pl.program_id(0)
    tm = x_ref.shape[0]
    g = g_ref[...]
    h_ext[0:HALO_ROWS, :] = jnp.where(i % tiles_per_seq != 0, _rms(xp_ref[...], g), 0.0).astype(BF16)
    h_ext[HALO_ROWS:HALO_ROWS + tm, :] = _rms(x_ref[...], g).astype(BF16)
    h_ext[HALO_ROWS + tm:, :] = jnp.where((i + 1) % tiles_per_seq != 0, _rms(xn_ref[...], g), 0.0).astype(BF16)
    blk_rows = CONV_ROWS + 2 * SUBLANES

    def finish(n0, main):
        conv_col = _conv_column(n0)
        if conv_col is None:
            if n0 < (COL_Z + 1) * D_INNER:
                main = _silu(main)
            o_ref[:, n0:n0 + MATMUL_N_CHUNK] = main.astype(BF16)
            return
        for r0 in range(0, tm, CONV_ROWS):
            for l0 in range(0, MATMUL_N_CHUNK, LANES):
                cc = slice(conv_col + l0, conv_col + l0 + LANES)
                blk = ext[HALO_ROWS - SUBLANES + r0:HALO_ROWS - SUBLANES + r0 + blk_rows, l0:l0 + LANES]
                acc = cb_ref[:, cc] + cw_ref[CONV_PAD:CONV_PAD + 1, cc] * blk[SUBLANES:SUBLANES + CONV_ROWS]
                for t in range(CONV_K):
                    if t != CONV_PAD:
                        shifted = pltpu.roll(blk, (CONV_PAD - t) % blk_rows, axis=0)
                        acc = acc + cw_ref[t:t + 1, cc] * shifted[SUBLANES:SUBLANES + CONV_ROWS]
                o_ref[r0:r0 + CONV_ROWS, n0 + l0:n0 + l0 + LANES] = _silu(acc).astype(BF16)

    pending = None
    for n0 in _chunk_order():
        cols = slice(n0, n0 + MATMUL_N_CHUNK)
        if _conv_column(n0) is None:
            main = _dot(h_ext[HALO_ROWS:HALO_ROWS + tm, :], w_ref[:, cols])
        else:
            ext[...] = _dot(h_ext[...], w_ref[:, cols])
            main = None
        if pending is not None:
            finish(*pending)
        pending = (n0, main)
    dt_ref[...] = _dot(h_ext[HALO_ROWS:HALO_ROWS + tm, :], wdt_ref[...])
    finish(*pending)


def _chunk_order():
    chunks = list(range(0, PROJ_WIDTH, MATMUL_N_CHUNK))
    heavy = [n0 for n0 in chunks if _conv_column(n0) is not None]
    light = [n0 for n0 in chunks if _conv_column(n0) is None]
    order = []
    while heavy or light:
        if heavy:
            order.append(heavy.pop(0))
        take = -(-len(light) // (len(heavy) + 1))
        order.extend(light[:take])
        light = light[take:]
    return order


def _conv_column(n0):
    if COL_XS * D_INNER <= n0 < (COL_XS + 1) * D_INNER:
        return n0 - COL_XS * D_INNER
    if COL_BC * BC_WIDTH <= n0 < (COL_BC + 1) * BC_WIDTH:
        return D_INNER + n0 - COL_BC * BC_WIDTH
    return None


def _in_proj(x2d, w, tm, seq):
    m = x2d.shape[0]
    halo_per_tile = tm // HALO_ROWS
    n_halo = m // HALO_ROWS
    return pl.pallas_call(
        functools.partial(_in_proj_kernel, seq // tm),
        grid=(m // tm,),
        in_specs=[pl.BlockSpec((tm, D_MODEL), lambda i: (i, 0)),
                  pl.BlockSpec((HALO_ROWS, D_MODEL), lambda i: (jnp.maximum(i * halo_per_tile - 1, 0), 0)),
                  pl.BlockSpec((HALO_ROWS, D_MODEL),
                               lambda i: (jnp.minimum((i + 1) * halo_per_tile, n_halo - 1), 0)),
                  _resident((1, D_MODEL)),
                  _resident((D_MODEL, PROJ_WIDTH)),
                  _resident((D_MODEL, DT_PAD)),
                  _resident((CONV_K, D_INNER + BC_WIDTH)), _resident((1, D_INNER + BC_WIDTH))],
        out_specs=[pl.BlockSpec((tm, PROJ_WIDTH), lambda i: (i, 0)),
                   pl.BlockSpec((tm, DT_PAD), lambda i: (i, 0))],
        out_shape=[jax.ShapeDtypeStruct((m, PROJ_WIDTH), BF16),
                   jax.ShapeDtypeStruct((m, DT_PAD), F32)],
        scratch_shapes=[pltpu.VMEM((tm + 2 * HALO_ROWS, D_MODEL), BF16),
                        pltpu.VMEM((tm + 2 * HALO_ROWS, MATMUL_N_CHUNK), F32)],
        compiler_params=_params(1),
        name="in_proj",
    )(x2d, x2d, x2d, w["norm_mix"], w["w_main"], w["w_dt"], w["conv_w"], w["conv_b"])


def _attn_kernel(n_sub, q_ref, kp_ref, kc_ref, kn_ref, vp_ref, vc_ref, vn_ref, sink_ref, *refs):
    bias_refs, o_ref = refs[:n_sub], refs[n_sub]
    k_all = jnp.concatenate([kp_ref[0], kc_ref[0], kn_ref[0]], axis=0)
    v_all = jnp.concatenate([vp_ref[0], vc_ref[0], vn_ref[0]], axis=0)
    vt_all = v_all.astype(F32).T.astype(BF16)
    cols = Q_PER_KV * BLOCK
    ones_rows = (lax.broadcasted_iota(jnp.int32, (HALO_ROWS, SPAN), 0) == 0).astype(BF16)
    for sub in range(n_sub):
        q = q_ref[0, sub * BLOCK:(sub + 1) * BLOCK, :] * jnp.asarray(HEAD_DIM ** -0.5, BF16)
        k = k_all[sub * BLOCK:sub * BLOCK + SPAN]
        v_t = vt_all[:, sub * BLOCK:sub * BLOCK + SPAN]
        bias_ref = bias_refs[sub]

        def logits(g):
            q4 = jnp.concatenate([q[:, (g * Q_PER_KV + r) * HEAD_DIM:(g * Q_PER_KV + r + 1) * HEAD_DIM]
                                  for r in range(Q_PER_KV)], axis=0)
            return (_dot_nt(k[:, g * HEAD_DIM:(g + 1) * HEAD_DIM], q4)
                    + bias_ref[0, g * SPAN:(g + 1) * SPAN, :])

        outs = []
        s_next = logits(0)
        for g in range(N_KV_HEADS):
            s = s_next
            if g + 1 < N_KV_HEADS:
                s_next = logits(g + 1)
            sink = sink_ref[:, g * cols:(g + 1) * cols]
            m = jnp.maximum(jnp.max(s, axis=0, keepdims=True), sink)
            e = jnp.exp(s - m).astype(BF16)
            v_aug = jnp.concatenate([v_t[g * HEAD_DIM:(g + 1) * HEAD_DIM], ones_rows], axis=0)
            o_aug = _dot(v_aug, e)
            den = o_aug[HEAD_DIM:HEAD_DIM + 1] + jnp.exp(sink - m)
            o_t = o_aug[:HEAD_DIM] * (1.0 / den)
            outs.extend(o_t[:, r * BLOCK:(r + 1) * BLOCK] for r in range(Q_PER_KV))
        o_ref[0, sub * BLOCK:(sub + 1) * BLOCK, :] = jnp.concatenate(outs, axis=0).T.astype(BF16)


def _window_attention(proj3d, bias, sink):
    b, s, _ = proj3d.shape
    nblk = s // BLOCK
    n_sub = ATTN_BLOCKS_PER_STEP if nblk % ATTN_BLOCKS_PER_STEP == 0 else 1
    nstep = nblk // n_sub

    def edge_spec(col, index):
        return pl.BlockSpec((1, BLOCK, KV_WIDTH), lambda bi, i: (bi, index(i), col))

    def kv_specs(col):
        return [edge_spec(col, lambda i: jnp.maximum(i * n_sub - 1, 0)),
                pl.BlockSpec((1, n_sub * BLOCK, KV_WIDTH), lambda bi, i: (bi, i, col)),
                edge_spec(col, lambda i: jnp.minimum((i + 1) * n_sub, nblk - 1))]

    def bias_spec(sub):
        def variant(bi, i):
            first = (i * n_sub + sub == 0).astype(jnp.int32)
            last = (i * n_sub + sub == nblk - 1).astype(jnp.int32)
            return (first + 2 * last, 0, 0)
        return pl.BlockSpec((1, N_KV_HEADS * SPAN, Q_PER_KV * BLOCK), variant)

    return pl.pallas_call(
        functools.partial(_attn_kernel, n_sub),
        grid=(b, nstep),
        in_specs=[pl.BlockSpec((1, n_sub * BLOCK, ATTN_WIDTH), lambda bi, i: (bi, i, COL_Q)),
                  *kv_specs(COL_K), *kv_specs(COL_V), _resident((1, N_HEADS * BLOCK)),
                  *[bias_spec(sub) for sub in range(n_sub)]],
        out_specs=pl.BlockSpec((1, n_sub * BLOCK, ATTN_WIDTH), lambda bi, i: (bi, i, 0)),
        out_shape=jax.ShapeDtypeStruct((b, s, ATTN_WIDTH), BF16),
        compiler_params=_params(2),
        name="window_attention",
    )(proj3d, proj3d, proj3d, proj3d, proj3d, proj3d, proj3d, sink, *([bias] * n_sub))


def _attention_call(proj3d, tables, w):
    return _window_attention(proj3d, tables["attn_bias"], w["attn_sink"])


def _split_hi_lo(v):
    hi = v.astype(BF16)
    return hi, (v - hi.astype(F32)).astype(BF16)


def _scan_tables(forward, dt_raw, dtb_ref, alog_ref, e2_ref):
    total_row = CHUNK - 1 if forward else 0
    dtv = _softplus(dt_raw + dtb_ref[...])
    adt = dtv * (-jnp.exp(alog_ref[...])) * LOG2_E
    ii = lax.broadcasted_iota(jnp.int32, (CHUNK, CHUNK), 0)
    jj = lax.broadcasted_iota(jnp.int32, (CHUNK, CHUNK), 1)
    scanned = (jj <= ii) if forward else (jj >= ii)
    tri = scanned.astype(BF16)
    a1 = adt.astype(BF16)
    r1 = adt - a1.astype(F32)
    a2 = r1.astype(BF16)
    a3 = (r1 - a2.astype(F32)).astype(BF16)
    pc = _dot(jnp.concatenate([tri, tri, tri], axis=1), jnp.concatenate([a1, a2, a3], axis=0))
    total = pc[total_row:total_row + 1, :]
    both = jnp.concatenate([jnp.exp2(pc), jnp.exp2(total - pc) * dtv], axis=0)
    hi, lo = _split_hi_lo(both)
    both_e = _dot(jnp.concatenate([hi, lo], axis=1), e2_ref[...])
    off_e = both_e[:CHUNK]
    w_e = both_e[CHUNK:]
    decay = off_e[total_row:total_row + 1, :]
    row_t = (pc - jnp.log(dtv) * LOG2_E).T
    return scanned, pc, row_t, off_e, w_e, decay


def _scan_group(g, lane0, scanned, pc, row_t, off_e, decay, b_t, c_bf, x_bf, xw_bf, state):
    gcols = slice(g * GROUP_WIDTH, (g + 1) * GROUP_WIDTH)
    gmat = _dot(c_bf, b_t)
    sg = state[g]
    y_off = _dot(c_bf, sg.astype(BF16)) * off_e[:, gcols]

    def decay_block(r):
        lane = lane0 + g * HEADS_PER_GROUP + r
        delta = pc[:, lane:lane + 1] - row_t[lane:lane + 1, :]
        return (gmat * jnp.exp2(jnp.where(scanned, delta, NEG_BIG))).astype(BF16)

    first_head = lax.broadcasted_iota(jnp.int32, (1, LANES), 1) < SSD_HEAD_DIM
    ys = []
    for q in range(HEADS_PER_GROUP // 2):
        x_pair = x_bf[:, q * LANES:(q + 1) * LANES]
        zero = jnp.zeros_like(x_pair)
        rhs = jnp.concatenate([jnp.where(first_head, x_pair, zero), jnp.where(first_head, zero, x_pair)],
                              axis=0)
        ys.append(_dot(jnp.concatenate([decay_block(2 * q), decay_block(2 * q + 1)], axis=1), rhs))
    state[g] = sg * decay[:, gcols] + _dot(b_t, xw_bf)
    return jnp.concatenate(ys, axis=-1) + y_off


def _ssd_bwd_kernel(x_ref, bc_ref, dt_ref, dtb_ref, alog_ref, e2_ref, y_ref, bt_ref, state):
    @pl.when(pl.program_id(1) == 0)
    def _():
        state[...] = jnp.zeros_like(state)

    order = list(reversed(range(x_ref.shape[1] // CHUNK)))
    tabs = {ci: _scan_tables(False, dt_ref[0, ci * CHUNK:(ci + 1) * CHUNK, :], dtb_ref, alog_ref, e2_ref)
            for ci in order}
    bc_half = BC_WIDTH // 2
    for ci in order:
        rows = slice(ci * CHUNK, (ci + 1) * CHUNK)
        scanned, pc, row_t, off_e, w_e, decay = tabs[ci]
        for g in range(SSD_GROUPS):
            gcols = slice(g * GROUP_WIDTH, (g + 1) * GROUP_WIDTH)
            x_bf = x_ref[0, rows, gcols]
            b_t = bc_ref[0, rows, g * D_STATE:(g + 1) * D_STATE].astype(F32).T.astype(BF16)
            bt_ref[0, ci * bc_half + g * D_STATE:ci * bc_half + (g + 1) * D_STATE, :] = b_t
            c_bf = bc_ref[0, rows, (SSD_GROUPS + g) * D_STATE:(SSD_GROUPS + g + 1) * D_STATE]
            y = _scan_group(g, SSD_HEADS, scanned, pc, row_t, off_e, decay, b_t, c_bf, x_bf,
                            (x_bf.astype(F32) * w_e[:, gcols]).astype(BF16), state)
            y_ref[0, rows, gcols] = y.astype(BF16)


def _ssd_fwd_kernel(x_ref, c_ref, bt_ref, dt_ref, dtb_ref, alog_ref, e2_ref, yb_ref, z_ref, dskip_ref,
                    gn_ref, out_ref, state, ybuf):
    @pl.when(pl.program_id(1) == 0)
    def _():
        state[...] = jnp.zeros_like(state)

    order = list(range(x_ref.shape[1] // CHUNK))
    tabs = {ci: _scan_tables(True, dt_ref[0, ci * CHUNK:(ci + 1) * CHUNK, :], dtb_ref, alog_ref, e2_ref)
            for ci in order}
    bc_half = BC_WIDTH // 2
    for ci in order:
        rows = slice(ci * CHUNK, (ci + 1) * CHUNK)
        scanned, pc, row_t, off_e, w_e, decay = tabs[ci]
        for g in range(SSD_GROUPS):
            gcols = slice(g * GROUP_WIDTH, (g + 1) * GROUP_WIDTH)
            x_bf = x_ref[0, rows, gcols]
            xg = x_bf.astype(F32)
            b_t = bt_ref[0, ci * bc_half + g * D_STATE:ci * bc_half + (g + 1) * D_STATE, :]
            c_bf = c_ref[0, rows, g * D_STATE:(g + 1) * D_STATE]
            y = _scan_group(g, 0, scanned, pc, row_t, off_e, decay, b_t, c_bf, x_bf,
                            (xg * w_e[:, gcols]).astype(BF16), state)
            ybuf[rows, gcols] = y + yb_ref[0, rows, gcols].astype(F32) + xg * dskip_ref[:, gcols]
        gated = ybuf[rows, :] * z_ref[0, rows, :].astype(F32)
        out_ref[0, rows, :] = _rms(gated, gn_ref[...]).astype(BF16)


def _ssd_call(proj3d, dt3d, tables, w):
    b, s, _ = proj3d.shape
    nchunk = s // CHUNK
    per_step = SSD_CHUNKS_PER_STEP if nchunk % SSD_CHUNKS_PER_STEP == 0 else 1
    nstep = nchunk // per_step
    rows = per_step * CHUNK
    bc_half = BC_WIDTH // 2

    def rev_spec(width, col=0):
        return pl.BlockSpec((1, rows, width), lambda bi, c: (bi, nstep - 1 - c, col))

    def chunk_spec(width, col=0):
        return pl.BlockSpec((1, rows, width), lambda bi, c: (bi, c, col))

    state = pltpu.VMEM((SSD_GROUPS, D_STATE, GROUP_WIDTH), F32)
    y_bwd, bt = pl.pallas_call(
        _ssd_bwd_kernel,
        grid=(b, nstep),
        in_specs=[rev_spec(D_INNER, COL_XS), rev_spec(BC_WIDTH, COL_BC), rev_spec(DT_PAD),
                  _resident((1, DT_PAD)), _resident((1, DT_PAD)), _resident((2 * DT_PAD, D_INNER))],
        out_specs=[rev_spec(D_INNER),
                   pl.BlockSpec((1, per_step * bc_half, CHUNK), lambda bi, c: (bi, nstep - 1 - c, 0))],
        out_shape=[jax.ShapeDtypeStruct((b, s, D_INNER), BF16),
                   jax.ShapeDtypeStruct((b, nchunk * bc_half, CHUNK), BF16)],
        scratch_shapes=[state],
        compiler_params=_params(2),
        name="ssd_backward",
    )(proj3d, proj3d, dt3d, w["dt_bias"], w["a_log"], tables["expand_bwd"])

    return pl.pallas_call(
        _ssd_fwd_kernel,
        grid=(b, nstep),
        in_specs=[chunk_spec(D_INNER, COL_XS), chunk_spec(bc_half, COL_BC * 2 + 1),
                  pl.BlockSpec((1, per_step * bc_half, CHUNK), lambda bi, c: (bi, c, 0)),
                  chunk_spec(DT_PAD),
                  _resident((1, DT_PAD)), _resident((1, DT_PAD)), _resident((2 * DT_PAD, D_INNER)),
                  chunk_spec(D_INNER), chunk_spec(D_INNER, COL_Z),
                  _resident((1, D_INNER)), _resident((1, D_INNER))],
        out_specs=chunk_spec(D_INNER),
        out_shape=jax.ShapeDtypeStruct((b, s, D_INNER), BF16),
        scratch_shapes=[state, pltpu.VMEM((rows, D_INNER), F32)],
        compiler_params=_params(2),
        name="ssd_forward",
    )(proj3d, proj3d, bt, dt3d, w["dt_bias"], w["a_log"], tables["expand_fwd"], y_bwd, proj3d,
      w["d_skip"], w["norm_ssd"])


def _mem_kv_kernel(mem_ref, g_ref, w_ref, o_ref):
    h = _rms(mem_ref[0], g_ref[...]).astype(BF16)
    o_ref[0] = _dot(h, w_ref[...]).astype(BF16)


def _mem_kv(mem, g, w_kv):
    b = mem.shape[0]
    return pl.pallas_call(
        _mem_kv_kernel,
        grid=(b,),
        in_specs=[pl.BlockSpec((1, MEM_LEN, D_MODEL), lambda i: (i, 0, 0)),
                  _resident((1, D_MODEL)), _resident((D_MODEL, 2 * D_MODEL))],
        out_specs=pl.BlockSpec((1, MEM_LEN, 2 * D_MODEL), lambda i: (i, 0, 0)),
        out_shape=jax.ShapeDtypeStruct((b, MEM_LEN, 2 * D_MODEL), BF16),
        compiler_params=_params(1),
        name="memory_kv",
    )(mem, g, w_kv)


def _merge_cross_kernel(ssd_ref, att_ref, gate_ref, x_ref, kv_ref, wpa_ref, wps_ref, wout_ref,
                        gc_ref, wq_ref, wo_ref, o_ref):
    gates = _sigmoid(gate_ref[...].astype(F32))
    mixed = (gates[:, :D_MODEL] * _dot(att_ref[...], wpa_ref[...])
             + gates[:, D_MODEL:] * _dot(ssd_ref[...], wps_ref[...]))
    x1 = x_ref[...] + _dot(mixed.astype(BF16), wout_ref[...])
    h = _rms(x1, gc_ref[...]).astype(BF16)
    q = (_dot(h, wq_ref[...]) * CROSS_HEAD_DIM ** -0.5).astype(BF16)
    outs = []
    for hh in range(CROSS_HEADS):
        hs = slice(hh * CROSS_HEAD_DIM, (hh + 1) * CROSS_HEAD_DIM)
        vs = slice(D_MODEL + hh * CROSS_HEAD_DIM, D_MODEL + (hh + 1) * CROSS_HEAD_DIM)
        s = _dot_nt(q[:, hs], kv_ref[0, :, hs])
        e = jnp.exp(s - jnp.max(s, axis=-1, keepdims=True))
        den = jnp.sum(e, axis=-1, keepdims=True)
        outs.append(_dot(e.astype(BF16), kv_ref[0, :, vs]) * (1.0 / den))
    o = jnp.concatenate(outs, axis=-1).astype(BF16)
    o_ref[...] = x1 + _dot(o, wo_ref[...])


def _merge_cross(ssd_o, attn_o, proj2d, x2d, kv, w, tm, seq):
    m = x2d.shape[0]
    tiles_per_seq = seq // tm
    return pl.pallas_call(
        _merge_cross_kernel,
        grid=(m // tm,),
        in_specs=[pl.BlockSpec((tm, D_INNER), lambda i: (i, 0)),
                  pl.BlockSpec((tm, ATTN_WIDTH), lambda i: (i, 0)),
                  pl.BlockSpec((tm, 2 * D_MODEL), lambda i: (i, COL_GATES)),
                  pl.BlockSpec((tm, D_MODEL), lambda i: (i, 0)),
                  pl.BlockSpec((1, MEM_LEN, 2 * D_MODEL), lambda i: (i // tiles_per_seq, 0, 0)),
                  _resident((ATTN_WIDTH, D_MODEL)), _resident((D_INNER, D_MODEL)),
                  _resident((D_MODEL, D_MODEL)), _resident((1, D_MODEL)),
                  _resident((D_MODEL, D_MODEL)), _resident((D_MODEL, D_MODEL))],
        out_specs=pl.BlockSpec((tm, D_MODEL), lambda i: (i, 0)),
        out_shape=jax.ShapeDtypeStruct((m, D_MODEL), F32),
        compiler_params=_params(1),
        name="merge_cross",
    )(ssd_o, attn_o, proj2d, x2d, kv, w["w_proj_attn"], w["w_proj_ssd"], w["w_out"],
      w["norm_cross"], w["w_q_cross"], w["w_o_cross"])


def _ffn_chunks():
    chunks, f0 = [], 0
    while f0 < D_FF:
        fw = min(MATMUL_N_CHUNK, D_FF - f0)
        chunks.append((f0, fw))
        f0 += fw
    return chunks


def _ffn_kernel(final_norm, x_ref, g_ref, wgu_ref, wd_ref, gf_ref, o_ref, acc_ref):
    x = x_ref[...]
    h = _rms(x, g_ref[...]).astype(BF16)
    for n, (f0, fw) in enumerate(_ffn_chunks()):
        gate = _dot(h, wgu_ref[:, f0:f0 + fw])
        up = _dot(h, wgu_ref[:, D_FF + f0:D_FF + f0 + fw])
        part = _dot((_silu(gate) * up).astype(BF16), wd_ref[f0:f0 + fw, :])
        if n == 0:
            acc_ref[...] = x + part
        else:
            acc_ref[...] += part
    y = acc_ref[...]
    o_ref[...] = _rms(y, gf_ref[...]) if final_norm else y


def _ffn(x2d, g, w_gate_up, w_down, g_final, final_norm, tm):
    m = x2d.shape[0]
    return pl.pallas_call(
        functools.partial(_ffn_kernel, final_norm),
        grid=(m // tm,),
        in_specs=[pl.BlockSpec((tm, D_MODEL), lambda i: (i, 0)),
                  _resident((1, D_MODEL)),
                  _resident((D_MODEL, 2 * D_FF)), _resident((D_FF, D_MODEL)),
                  _resident((1, D_MODEL))],
        out_specs=pl.BlockSpec((tm, D_MODEL), lambda i: (i, 0)),
        out_shape=jax.ShapeDtypeStruct((m, D_MODEL), F32),
        scratch_shapes=[pltpu.VMEM((tm, D_MODEL), F32)],
        compiler_params=_params(1),
        name="ffn_final" if final_norm else "ffn",
    )(x2d, g, w_gate_up, w_down, g_final)


def _t5_bucket(rel):
    nb = NUM_BUCKETS // 2
    max_exact = nb // 2
    ret = jnp.where(rel > 0, nb, 0)
    n = jnp.abs(rel)
    nf = jnp.maximum(n, 1).astype(F32)
    large = max_exact + (jnp.log(nf / max_exact) / math.log(MAX_DISTANCE / max_exact)
                         * (nb - max_exact)).astype(jnp.int32)
    large = jnp.minimum(large, nb - 1)
    return ret + jnp.where(n < max_exact, n, large)


def _attention_bias(rel_bias):
    rel = jnp.arange(SPAN)[None, :] - WINDOW - jnp.arange(BLOCK)[:, None]
    one_hot = (_t5_bucket(rel)[None] == jnp.arange(NUM_BUCKETS)[:, None, None]).astype(F32)
    bias = jnp.einsum("bh,bqk->hqk", rel_bias.astype(F32), one_hot, precision=lax.Precision.HIGHEST)
    bias = jnp.where((jnp.abs(rel) <= WINDOW)[None], bias, NEG_BIG)
    col = jnp.arange(SPAN)
    variants = []
    for first, last in ((False, False), (True, False), (False, True), (True, True)):
        pad = (first & (col < WINDOW)) | (last & (col >= WINDOW + BLOCK))
        masked = jnp.where(pad[None, None, :], NEG_BIG, bias).reshape(N_KV_HEADS, Q_PER_KV, BLOCK, SPAN)
        variants.append(jnp.transpose(masked, (0, 3, 1, 2)).reshape(N_KV_HEADS * SPAN, Q_PER_KV * BLOCK))
    return jnp.stack(variants)


def _head_expand_matrix(lane0):
    e = np.zeros((2 * DT_PAD, D_INNER), np.float32)
    for h in range(SSD_HEADS):
        e[lane0 + h, h * SSD_HEAD_DIM:(h + 1) * SSD_HEAD_DIM] = 1.0
        e[DT_PAD + lane0 + h, h * SSD_HEAD_DIM:(h + 1) * SSD_HEAD_DIM] = 1.0
    return jnp.asarray(e, BF16)


def _tables(rel_bias):
    return {"attn_bias": _attention_bias(rel_bias), "expand_fwd": _head_expand_matrix(0),
            "expand_bwd": _head_expand_matrix(SSD_HEADS)}


def _prepare_layer(l, p):
    w_in = p["w_in"][l]
    o_q, o_k, o_v = 0, ATTN_WIDTH, ATTN_WIDTH + KV_WIDTH
    o_z = ATTN_WIDTH + 2 * KV_WIDTH
    o_xs = o_z + D_INNER
    o_bc = o_xs + D_INNER
    o_dt = o_bc + BC_WIDTH
    o_g = o_dt + 2 * SSD_HEADS
    w_main = jnp.concatenate([w_in[:, o_z:o_xs], w_in[:, o_xs:o_bc], w_in[:, o_g:o_g + 2 * D_MODEL],
                              w_in[:, o_q:o_k], w_in[:, o_bc:o_dt], w_in[:, o_k:o_v],
                              w_in[:, o_v:o_z]], axis=1).astype(BF16)
    w_dt = jnp.pad(w_in[:, o_dt:o_g], ((0, 0), (0, DT_PAD - 2 * SSD_HEADS))).astype(BF16)
    pad_dt = lambda a: jnp.pad(a.reshape(1, 2 * SSD_HEADS), ((0, 0), (0, DT_PAD - 2 * SSD_HEADS)))
    row = lambda a: a.reshape(1, -1).astype(F32)
    return {
        "norm_mix": row(p["norm_mix"][l]), "w_main": w_main, "w_dt": w_dt,
        "attn_sink": jnp.repeat(p["attn_sink"][l].astype(F32), BLOCK).reshape(1, N_HEADS * BLOCK),
        "conv_w": p["conv_w"][l].astype(F32), "conv_b": row(p["conv_b"][l]),
        "dt_bias": pad_dt(p["dt_bias"][l]), "a_log": pad_dt(p["a_log"][l]),
        "d_skip": row(jnp.repeat(p["d_skip"][l], SSD_HEAD_DIM)),
        "norm_ssd": row(p["norm_ssd"][l]),
        "w_proj_attn": p["w_proj_attn"][l].astype(BF16), "w_proj_ssd": p["w_proj_ssd"][l].astype(BF16),
        "w_out": p["w_out"][l].astype(BF16), "norm_cross": row(p["norm_cross"][l]),
        "norm_mem": row(p["norm_mem"][l]), "w_q_cross": p["w_q_cross"][l].astype(BF16),
        "w_kv_cross": p["w_kv_cross"][l].astype(BF16), "w_o_cross": p["w_o_cross"][l].astype(BF16),
        "norm_ffn": row(p["norm_ffn"][l]), "w_gate_up": p["w_gate_up"][l].astype(BF16),
        "w_down": p["w_down"][l].astype(BF16),
    }


def _encoder(x, mem, layers, tables, norm_final):
    b, s, _ = x.shape
    m = b * s
    tm = min(ROW_TILE, s)
    x2d = x.reshape(m, D_MODEL)
    for l, w in enumerate(layers):
        proj, dt = _in_proj(x2d, w, tm, s)
        proj3d = proj.reshape(b, s, PROJ_WIDTH)
        attn_o = _attention_call(proj3d, tables, w)
        ssd_o = _ssd_call(proj3d, dt.reshape(b, s, DT_PAD), tables, w)
        kv = _mem_kv(mem, w["norm_mem"], w["w_kv_cross"])
        x2d = _merge_cross(ssd_o.reshape(m, D_INNER), attn_o.reshape(m, ATTN_WIDTH), proj, x2d, kv, w,
                           tm, s)
        x2d = _ffn(x2d, w["norm_ffn"], w["w_gate_up"], w["w_down"], norm_final,
                   l == len(layers) - 1, tm)
    return x2d.reshape(b, s, D_MODEL)


def kernel(x_prompt, x_sample, mem_prompt, mem_sample, rel_bias, norm_mix, w_in, attn_sink, conv_w,
           conv_b, dt_bias, a_log, d_skip, norm_ssd, w_proj_attn, w_proj_ssd, w_out, norm_cross,
           norm_mem, w_q_cross, w_kv_cross, w_o_cross, norm_ffn, w_gate_up, w_down, norm_final):
    p = dict(norm_mix=norm_mix, w_in=w_in, attn_sink=attn_sink, conv_w=conv_w, conv_b=conv_b,
             dt_bias=dt_bias, a_log=a_log, d_skip=d_skip, norm_ssd=norm_ssd, w_proj_attn=w_proj_attn,
             w_proj_ssd=w_proj_ssd, w_out=w_out, norm_cross=norm_cross, norm_mem=norm_mem,
             w_q_cross=w_q_cross, w_kv_cross=w_kv_cross, w_o_cross=w_o_cross, norm_ffn=norm_ffn,
             w_gate_up=w_gate_up, w_down=w_down)
    layers = [_prepare_layer(l, p) for l in range(norm_mix.shape[0])]
    tables = _tables(rel_bias)
    g_final = norm_final.reshape(1, D_MODEL).astype(F32)
    y_prompt = _encoder(x_prompt, mem_prompt, layers, tables, g_final)
    y_sample = _encoder(x_sample, mem_sample, layers, tables, g_final)
    return (y_prompt, y_sample)
```

```python
import functools
import itertools
import math

import numpy as np
import jax
import jax.numpy as jnp
from jax import lax
from jax.experimental import pallas as pl
from jax.experimental.pallas import tpu as pltpu

F32 = jnp.float32
BF16 = jnp.bfloat16

D_MODEL = 1024
N_HEADS = 16
N_KV_HEADS = 4
HEAD_DIM = 64
Q_PER_KV = N_HEADS // N_KV_HEADS
ATTN_WIDTH = N_HEADS * HEAD_DIM
KV_WIDTH = N_KV_HEADS * HEAD_DIM
WINDOW = 128
BLOCK = 128
SPAN = BLOCK + 2 * WINDOW
NUM_BUCKETS = 32
MAX_DISTANCE = 128
D_INNER = 2 * D_MODEL
SSD_HEAD_DIM = 64
SSD_HEADS = D_INNER // SSD_HEAD_DIM
SSD_GROUPS = 4
HEADS_PER_GROUP = SSD_HEADS // SSD_GROUPS
GROUP_WIDTH = HEADS_PER_GROUP * SSD_HEAD_DIM
D_STATE = 128
BC_WIDTH = 2 * SSD_GROUPS * D_STATE
CONV_K = 5
CONV_PAD = CONV_K // 2
CHUNK = 128
MEM_LEN = 256
CROSS_HEADS = 4
CROSS_HEAD_DIM = D_MODEL // CROSS_HEADS
D_FF = ((8 * D_MODEL // 3 + 255) // 256) * 256
EPS = 1e-6
LOG2_E = math.log2(math.e)

PROJ_WIDTH = 2 * D_INNER + 2 * D_MODEL + ATTN_WIDTH + BC_WIDTH + 2 * KV_WIDTH
COL_Z = 0
COL_XS = 1
COL_GATES = 2
COL_Q = (3 * D_INNER) // ATTN_WIDTH
COL_BC = (3 * D_INNER + ATTN_WIDTH) // BC_WIDTH
COL_K = (3 * D_INNER + ATTN_WIDTH + BC_WIDTH) // KV_WIDTH
COL_V = COL_K + 1
DT_PAD = 128

LANES = 128
SUBLANES = 8
HALO_ROWS = 16
CONV_ROWS = 64
SSD_CHUNKS_PER_STEP = 4
ATTN_BLOCKS_PER_STEP = 4
MATMUL_N_CHUNK = 512
IN_PROJ_N_CHUNK = 256
ROW_TILE = 512
NEG_BIG = -1e30
VMEM_LIMIT = 56 * 1024 * 1024


def _rms(x, g):
    return x * lax.rsqrt(jnp.mean(x * x, axis=-1, keepdims=True) + EPS) * g


def _silu(x):
    return x * lax.logistic(x)


def _sigmoid(x):
    return lax.logistic(x)


def _softplus(x):
    return jnp.maximum(x, 0.0) + jnp.log1p(jnp.exp(-jnp.abs(x)))


def _dot(a, b):
    return jnp.dot(a, b, preferred_element_type=F32)


def _dot_nt(a, b):
    return lax.dot_general(a, b, (((1,), (1,)), ((), ())), preferred_element_type=F32)


def _resident(shape):
    nd = len(shape)
    return pl.BlockSpec(shape, lambda *_: (0,) * nd, pipeline_mode=pl.Buffered(1))


def _params(n_axes):
    return pltpu.CompilerParams(dimension_semantics=("arbitrary",) * n_axes,
                                vmem_limit_bytes=VMEM_LIMIT)


def _in_proj_kernel(tiles_per_seq, x_ref, xp_ref, xn_ref, g_ref, w_ref, wdt_ref, cw_ref, cb_ref,
                    o_ref, dt_ref, h_ext, ext):
    i = pl.program_id(0)
    tm = x_ref.shape[0]
    g = g_ref[...]
    h_ext[0:HALO_ROWS, :] = jnp.where(i % tiles_per_seq != 0, _rms(xp_ref[...], g), 0.0).astype(BF16)
    h_ext[HALO_ROWS:HALO_ROWS + tm, :] = _rms(x_ref[...], g).astype(BF16)
    h_ext[HALO_ROWS + tm:, :] = jnp.where((i + 1) % tiles_per_seq != 0, _rms(xn_ref[...], g), 0.0).astype(BF16)
    blk_rows = CONV_ROWS + 2 * SUBLANES

    def finish(n0, main):
        conv_col = _conv_column(n0)
        if conv_col is None:
            if n0 < (COL_Z + 1) * D_INNER:
                main = _silu(main)
            o_ref[:, n0:n0 + IN_PROJ_N_CHUNK] = main.astype(BF16)
            return
        for r0 in range(0, tm, CONV_ROWS):
            for l0 in range(0, IN_PROJ_N_CHUNK, LANES):
                cc = slice(conv_col + l0, conv_col + l0 + LANES)
                blk = ext[HALO_ROWS - SUBLANES + r0:HALO_ROWS - SUBLANES + r0 + blk_rows, l0:l0 + LANES]
                acc = cb_ref[:, cc] + cw_ref[CONV_PAD:CONV_PAD + 1, cc] * blk[SUBLANES:SUBLANES + CONV_ROWS]
                for t in range(CONV_K):
                    if t != CONV_PAD:
                        shifted = pltpu.roll(blk, (CONV_PAD - t) % blk_rows, axis=0)
                        acc = acc + cw_ref[t:t + 1, cc] * shifted[SUBLANES:SUBLANES + CONV_ROWS]
                o_ref[r0:r0 + CONV_ROWS, n0 + l0:n0 + l0 + LANES] = _silu(acc).astype(BF16)

    pending = None
    for n0 in _chunk_order():
        cols = slice(n0, n0 + IN_PROJ_N_CHUNK)
        if _conv_column(n0) is None:
            main = _dot(h_ext[HALO_ROWS:HALO_ROWS + tm, :], w_ref[:, cols])
        else:
            ext[...] = _dot(h_ext[...], w_ref[:, cols])
            main = None
        if pending is not None:
            finish(*pending)
        pending = (n0, main)
    dt_ref[...] = _dot(h_ext[HALO_ROWS:HALO_ROWS + tm, :], wdt_ref[...])
    finish(*pending)


def _chunk_order():
    chunks = list(range(0, PROJ_WIDTH, IN_PROJ_N_CHUNK))
    heavy = [n0 for n0 in chunks if _conv_column(n0) is not None]
    light = [n0 for n0 in chunks if _conv_column(n0) is None]
    order = []
    while heavy or light:
        if heavy:
            order.append(heavy.pop(0))
        take = -(-len(light) // (len(heavy) + 1))
        order.extend(light[:take])
        light = light[take:]
    return order


def _conv_column(n0):
    if COL_XS * D_INNER <= n0 < (COL_XS + 1) * D_INNER:
        return n0 - COL_XS * D_INNER
    if COL_BC * BC_WIDTH <= n0 < (COL_BC + 1) * BC_WIDTH:
        return D_INNER + n0 - COL_BC * BC_WIDTH
    return None


def _in_proj(x2d, w, tm, seq):
    m = x2d.shape[0]
    halo_per_tile = tm // HALO_ROWS
    n_halo = m // HALO_ROWS
    return pl.pallas_call(
        functools.partial(_in_proj_kernel, seq // tm),
        grid=(m // tm,),
        in_specs=[pl.BlockSpec((tm, D_MODEL), lambda i: (i, 0)),
                  pl.BlockSpec((HALO_ROWS, D_MODEL), lambda i: (jnp.maximum(i * halo_per_tile - 1, 0), 0)),
                  pl.BlockSpec((HALO_ROWS, D_MODEL),
                               lambda i: (jnp.minimum((i + 1) * halo_per_tile, n_halo - 1), 0)),
                  _resident((1, D_MODEL)),
                  _resident((D_MODEL, PROJ_WIDTH)),
                  _resident((D_MODEL, DT_PAD)),
                  _resident((CONV_K, D_INNER + BC_WIDTH)), _resident((1, D_INNER + BC_WIDTH))],
        out_specs=[pl.BlockSpec((tm, PROJ_WIDTH), lambda i: (i, 0)),
                   pl.BlockSpec((tm, DT_PAD), lambda i: (i, 0))],
        out_shape=[jax.ShapeDtypeStruct((m, PROJ_WIDTH), BF16),
                   jax.ShapeDtypeStruct((m, DT_PAD), F32)],
        scratch_shapes=[pltpu.VMEM((tm + 2 * HALO_ROWS, D_MODEL), BF16),
                        pltpu.VMEM((tm + 2 * HALO_ROWS, IN_PROJ_N_CHUNK), F32)],
        compiler_params=_params(1),
        name="in_proj",
    )(x2d, x2d, x2d, w["norm_mix"], w["w_main"], w["w_dt"], w["conv_w"], w["conv_b"])


def _attn_kernel(n_sub, *refs):
    for _ in _attn_steps(n_sub, *refs):
        pass


def _attn_steps(n_sub, q_ref, kp_ref, kc_ref, kn_ref, vp_ref, vc_ref, vn_ref, sink_ref, *refs):
    bias_refs, o_ref = refs[:n_sub], refs[n_sub]
    k_all = jnp.concatenate([kp_ref[0], kc_ref[0], kn_ref[0]], axis=0)
    v_all = jnp.concatenate([vp_ref[0], vc_ref[0], vn_ref[0]], axis=0)
    vt_all = v_all.astype(F32).T.astype(BF16)
    cols = Q_PER_KV * BLOCK
    ones_rows = (lax.broadcasted_iota(jnp.int32, (HALO_ROWS, SPAN), 0) == 0).astype(BF16)
    for sub in range(n_sub):
        q = q_ref[0, sub * BLOCK:(sub + 1) * BLOCK, :] * jnp.asarray(HEAD_DIM ** -0.5, BF16)
        k = k_all[sub * BLOCK:sub * BLOCK + SPAN]
        v_t = vt_all[:, sub * BLOCK:sub * BLOCK + SPAN]
        bias_ref = bias_refs[sub]

        def logits(g):
            q4 = jnp.concatenate([q[:, (g * Q_PER_KV + r) * HEAD_DIM:(g * Q_PER_KV + r + 1) * HEAD_DIM]
                                  for r in range(Q_PER_KV)], axis=0)
            return (_dot_nt(k[:, g * HEAD_DIM:(g + 1) * HEAD_DIM], q4)
                    + bias_ref[0, g * SPAN:(g + 1) * SPAN, :])

        outs = []
        s_next = logits(0)
        for g in range(N_KV_HEADS):
            s = s_next
            if g + 1 < N_KV_HEADS:
                s_next = logits(g + 1)
            sink = sink_ref[:, g * cols:(g + 1) * cols]
            m = jnp.maximum(jnp.max(s, axis=0, keepdims=True), sink)
            e = jnp.exp(s - m).astype(BF16)
            v_aug = jnp.concatenate([v_t[g * HEAD_DIM:(g + 1) * HEAD_DIM], ones_rows], axis=0)
            o_aug = _dot(v_aug, e)
            den = o_aug[HEAD_DIM:HEAD_DIM + 1] + jnp.exp(sink - m)
            o_t = o_aug[:HEAD_DIM] * (1.0 / den)
            outs.extend(o_t[:, r * BLOCK:(r + 1) * BLOCK] for r in range(Q_PER_KV))
            if g + 1 < N_KV_HEADS:
                yield
        o_ref[0, sub * BLOCK:(sub + 1) * BLOCK, :] = jnp.concatenate(outs, axis=0).T.astype(BF16)
        yield


def _window_attention(proj3d, bias, sink):
    b, s, _ = proj3d.shape
    nblk = s // BLOCK
    n_sub = ATTN_BLOCKS_PER_STEP if nblk % ATTN_BLOCKS_PER_STEP == 0 else 1
    nstep = nblk // n_sub
    in_specs, out_spec = _attention_specs(nblk, n_sub, lambda i: i)
    return pl.pallas_call(
        functools.partial(_attn_kernel, n_sub),
        grid=(b, nstep),
        in_specs=in_specs,
        out_specs=out_spec,
        out_shape=jax.ShapeDtypeStruct((b, s, ATTN_WIDTH), BF16),
        compiler_params=_params(2),
        name="window_attention",
    )(*_attention_operands(proj3d, bias, sink, n_sub))


def _attention_specs(nblk, n_sub, pos):
    def edge_spec(col, index):
        return pl.BlockSpec((1, BLOCK, KV_WIDTH), lambda bi, i: (bi, index(pos(i)), col))

    def kv_specs(col):
        return [edge_spec(col, lambda p: jnp.maximum(p * n_sub - 1, 0)),
                pl.BlockSpec((1, n_sub * BLOCK, KV_WIDTH), lambda bi, i: (bi, pos(i), col)),
                edge_spec(col, lambda p: jnp.minimum((p + 1) * n_sub, nblk - 1))]

    def bias_spec(sub):
        def variant(bi, i):
            first = (pos(i) * n_sub + sub == 0).astype(jnp.int32)
            last = (pos(i) * n_sub + sub == nblk - 1).astype(jnp.int32)
            return (first + 2 * last, 0, 0)
        return pl.BlockSpec((1, N_KV_HEADS * SPAN, Q_PER_KV * BLOCK), variant)

    in_specs = [pl.BlockSpec((1, n_sub * BLOCK, ATTN_WIDTH), lambda bi, i: (bi, pos(i), COL_Q)),
                *kv_specs(COL_K), *kv_specs(COL_V), _resident((1, N_HEADS * BLOCK)),
                *[bias_spec(sub) for sub in range(n_sub)]]
    return in_specs, pl.BlockSpec((1, n_sub * BLOCK, ATTN_WIDTH), lambda bi, i: (bi, pos(i), 0))


def _attention_operands(proj3d, bias, sink, n_sub):
    return [proj3d] * 7 + [sink] + [bias] * n_sub


def _split_hi_lo(v):
    hi = v.astype(BF16)
    return hi, (v - hi.astype(F32)).astype(BF16)


def _scan_tables(forward, dt_raw, dtb_ref, alog_ref, e2_ref):
    total_row = CHUNK - 1 if forward else 0
    dtv = _softplus(dt_raw + dtb_ref[...])
    adt = dtv * (-jnp.exp(alog_ref[...])) * LOG2_E
    ii = lax.broadcasted_iota(jnp.int32, (CHUNK, CHUNK), 0)
    jj = lax.broadcasted_iota(jnp.int32, (CHUNK, CHUNK), 1)
    scanned = (jj <= ii) if forward else (jj >= ii)
    tri = scanned.astype(BF16)
    a1 = adt.astype(BF16)
    r1 = adt - a1.astype(F32)
    a2 = r1.astype(BF16)
    a3 = (r1 - a2.astype(F32)).astype(BF16)
    pc = _dot(jnp.concatenate([tri, tri, tri], axis=1), jnp.concatenate([a1, a2, a3], axis=0))
    total = pc[total_row:total_row + 1, :]
    both = jnp.concatenate([jnp.exp2(pc), jnp.exp2(total - pc) * dtv], axis=0)
    hi, lo = _split_hi_lo(both)
    both_e = _dot(jnp.concatenate([hi, lo], axis=1), e2_ref[...])
    off_e = both_e[:CHUNK]
    w_e = both_e[CHUNK:]
    decay = off_e[total_row:total_row + 1, :]
    row_t = (pc - jnp.log(dtv) * LOG2_E).T
    return scanned, pc, row_t, off_e, w_e, decay


def _scan_group(g, lane0, scanned, pc, row_t, off_e, decay, b_t, c_bf, x_bf, xw_bf, state):
    gcols = slice(g * GROUP_WIDTH, (g + 1) * GROUP_WIDTH)
    gmat = _dot(c_bf, b_t)
    sg = state[g]
    y_off = _dot(c_bf, sg.astype(BF16)) * off_e[:, gcols]

    def decay_block(r):
        lane = lane0 + g * HEADS_PER_GROUP + r
        delta = pc[:, lane:lane + 1] - row_t[lane:lane + 1, :]
        return (gmat * jnp.exp2(jnp.where(scanned, delta, NEG_BIG))).astype(BF16)

    first_head = lax.broadcasted_iota(jnp.int32, (1, LANES), 1) < SSD_HEAD_DIM
    ys = []
    for q in range(HEADS_PER_GROUP // 2):
        x_pair = x_bf[:, q * LANES:(q + 1) * LANES]
        zero = jnp.zeros_like(x_pair)
        rhs = jnp.concatenate([jnp.where(first_head, x_pair, zero), jnp.where(first_head, zero, x_pair)],
                              axis=0)
        ys.append(_dot(jnp.concatenate([decay_block(2 * q), decay_block(2 * q + 1)], axis=1), rhs))
    state[g] = sg * decay[:, gcols] + _dot(b_t, xw_bf)
    return jnp.concatenate(ys, axis=-1) + y_off


def _reset_state_at_row_start(state):
    @pl.when(pl.program_id(1) == 0)
    def _():
        state[...] = jnp.zeros_like(state)


def _ssd_bwd_kernel(*refs):
    _reset_state_at_row_start(refs[-1])
    for _ in _ssd_bwd_steps(*refs):
        pass


def _ssd_bwd_steps(x_ref, bc_ref, dt_ref, dtb_ref, alog_ref, e2_ref, y_ref, bt_ref, state):
    order = list(reversed(range(x_ref.shape[1] // CHUNK)))
    tabs = {ci: _scan_tables(False, dt_ref[0, ci * CHUNK:(ci + 1) * CHUNK, :], dtb_ref, alog_ref, e2_ref)
            for ci in order}
    bc_half = BC_WIDTH // 2
    for ci in order:
        rows = slice(ci * CHUNK, (ci + 1) * CHUNK)
        scanned, pc, row_t, off_e, w_e, decay = tabs[ci]
        for g in range(SSD_GROUPS):
            gcols = slice(g * GROUP_WIDTH, (g + 1) * GROUP_WIDTH)
            x_bf = x_ref[0, rows, gcols]
            b_t = bc_ref[0, rows, g * D_STATE:(g + 1) * D_STATE].astype(F32).T.astype(BF16)
            bt_ref[0, ci * bc_half + g * D_STATE:ci * bc_half + (g + 1) * D_STATE, :] = b_t
            c_bf = bc_ref[0, rows, (SSD_GROUPS + g) * D_STATE:(SSD_GROUPS + g + 1) * D_STATE]
            y = _scan_group(g, SSD_HEADS, scanned, pc, row_t, off_e, decay, b_t, c_bf, x_bf,
                            (x_bf.astype(F32) * w_e[:, gcols]).astype(BF16), state)
            y_ref[0, rows, gcols] = y.astype(BF16)
            yield


def _ssd_fwd_kernel(x_ref, c_ref, bt_ref, dt_ref, dtb_ref, alog_ref, e2_ref, yb_ref, z_ref, dskip_ref,
                    gn_ref, out_ref, state, ybuf):
    @pl.when(pl.program_id(1) == 0)
    def _():
        state[...] = jnp.zeros_like(state)

    order = list(range(x_ref.shape[1] // CHUNK))
    tabs = {ci: _scan_tables(True, dt_ref[0, ci * CHUNK:(ci + 1) * CHUNK, :], dtb_ref, alog_ref, e2_ref)
            for ci in order}
    bc_half = BC_WIDTH // 2
    for ci in order:
        rows = slice(ci * CHUNK, (ci + 1) * CHUNK)
        scanned, pc, row_t, off_e, w_e, decay = tabs[ci]
        for g in range(SSD_GROUPS):
            gcols = slice(g * GROUP_WIDTH, (g + 1) * GROUP_WIDTH)
            x_bf = x_ref[0, rows, gcols]
            xg = x_bf.astype(F32)
            b_t = bt_ref[0, ci * bc_half + g * D_STATE:ci * bc_half + (g + 1) * D_STATE, :]
            c_bf = c_ref[0, rows, g * D_STATE:(g + 1) * D_STATE]
            y = _scan_group(g, 0, scanned, pc, row_t, off_e, decay, b_t, c_bf, x_bf,
                            (xg * w_e[:, gcols]).astype(BF16), state)
            ybuf[rows, gcols] = y + yb_ref[0, rows, gcols].astype(F32) + xg * dskip_ref[:, gcols]
        gated = ybuf[rows, :] * z_ref[0, rows, :].astype(F32)
        out_ref[0, rows, :] = _rms(gated, gn_ref[...]).astype(BF16)


def _attn_ssd_bwd_kernel(n_sub, *refs):
    n_attn_in = 8 + n_sub
    n_ssd_in = 6
    o_ref, y_ref, bt_ref, state = refs[n_attn_in + n_ssd_in:]
    _reset_state_at_row_start(state)
    attn = _attn_steps(n_sub, *refs[:n_attn_in], o_ref)
    ssd = _ssd_bwd_steps(*refs[n_attn_in:n_attn_in + n_ssd_in], y_ref, bt_ref, state)
    for _ in itertools.zip_longest(attn, ssd):
        pass


def _mixer_calls(proj3d, dt3d, tables, w):
    b, s, _ = proj3d.shape
    nchunk = s // CHUNK
    per_step = SSD_CHUNKS_PER_STEP if nchunk % SSD_CHUNKS_PER_STEP == 0 else 1
    nstep = nchunk // per_step
    rows = per_step * CHUNK
    bc_half = BC_WIDTH // 2
    fused = per_step == ATTN_BLOCKS_PER_STEP and BLOCK == CHUNK

    def rev_spec(width, col=0):
        return pl.BlockSpec((1, rows, width), lambda bi, c: (bi, nstep - 1 - c, col))

    def chunk_spec(width, col=0):
        return pl.BlockSpec((1, rows, width), lambda bi, c: (bi, c, col))

    state = pltpu.VMEM((SSD_GROUPS, D_STATE, GROUP_WIDTH), F32)
    bwd_in_specs = [rev_spec(D_INNER, COL_XS), rev_spec(BC_WIDTH, COL_BC), rev_spec(DT_PAD),
                    _resident((1, DT_PAD)), _resident((1, DT_PAD)), _resident((2 * DT_PAD, D_INNER))]
    bwd_out_specs = [rev_spec(D_INNER),
                     pl.BlockSpec((1, per_step * bc_half, CHUNK), lambda bi, c: (bi, nstep - 1 - c, 0))]
    bwd_out_shape = [jax.ShapeDtypeStruct((b, s, D_INNER), BF16),
                     jax.ShapeDtypeStruct((b, nchunk * bc_half, CHUNK), BF16)]
    bwd_operands = [proj3d, proj3d, dt3d, w["dt_bias"], w["a_log"], tables["expand_bwd"]]
    if fused:
        attn_in_specs, attn_out_spec = _attention_specs(nchunk, per_step, lambda c: nstep - 1 - c)
        attn_o, y_bwd, bt = pl.pallas_call(
            functools.partial(_attn_ssd_bwd_kernel, per_step),
            grid=(b, nstep),
            in_specs=attn_in_specs + bwd_in_specs,
            out_specs=[attn_out_spec] + bwd_out_specs,
            out_shape=[jax.ShapeDtypeStruct((b, s, ATTN_WIDTH), BF16)] + bwd_out_shape,
            scratch_shapes=[state],
            compiler_params=_params(2),
            name="attention_ssd_backward",
        )(*_attention_operands(proj3d, tables["attn_bias"], w["attn_sink"], per_step), *bwd_operands)
    else:
        attn_o = _window_attention(proj3d, tables["attn_bias"], w["attn_sink"])
        y_bwd, bt = pl.pallas_call(
            _ssd_bwd_kernel,
            grid=(b, nstep),
            in_specs=bwd_in_specs,
            out_specs=bwd_out_specs,
            out_shape=bwd_out_shape,
            scratch_shapes=[state],
            compiler_params=_params(2),
            name="ssd_backward",
        )(*bwd_operands)

    ssd_o = pl.pallas_call(
        _ssd_fwd_kernel,
        grid=(b, nstep),
        in_specs=[chunk_spec(D_INNER, COL_XS), chunk_spec(bc_half, COL_BC * 2 + 1),
                  pl.BlockSpec((1, per_step * bc_half, CHUNK), lambda bi, c: (bi, c, 0)),
                  chunk_spec(DT_PAD),
                  _resident((1, DT_PAD)), _resident((1, DT_PAD)), _resident((2 * DT_PAD, D_INNER)),
                  chunk_spec(D_INNER), chunk_spec(D_INNER, COL_Z),
                  _resident((1, D_INNER)), _resident((1, D_INNER))],
        out_specs=chunk_spec(D_INNER),
        out_shape=jax.ShapeDtypeStruct((b, s, D_INNER), BF16),
        scratch_shapes=[state, pltpu.VMEM((rows, D_INNER), F32)],
        compiler_params=_params(2),
        name="ssd_forward",
    )(proj3d, proj3d, bt, dt3d, w["dt_bias"], w["a_log"], tables["expand_fwd"], y_bwd, proj3d,
      w["d_skip"], w["norm_ssd"])
    return attn_o, ssd_o


def _mem_kv_kernel(mem_ref, g_ref, w_ref, o_ref):
    h = _rms(mem_ref[0], g_ref[...]).astype(BF16)
    o_ref[0] = _dot(h, w_ref[...]).astype(BF16)


def _mem_kv(mem, g, w_kv):
    b = mem.shape[0]
    return pl.pallas_call(
        _mem_kv_kernel,
        grid=(b,),
        in_specs=[pl.BlockSpec((1, MEM_LEN, D_MODEL), lambda i: (i, 0, 0)),
                  _resident((1, D_MODEL)), _resident((D_MODEL, 2 * D_MODEL))],
        out_specs=pl.BlockSpec((1, MEM_LEN, 2 * D_MODEL), lambda i: (i, 0, 0)),
        out_shape=jax.ShapeDtypeStruct((b, MEM_LEN, 2 * D_MODEL), BF16),
        compiler_params=_params(1),
        name="memory_kv",
    )(mem, g, w_kv)


def _merge_cross_kernel(ssd_ref, att_ref, gate_ref, x_ref, kv_ref, wpa_ref, wps_ref, wout_ref,
                        gc_ref, wq_ref, wo_ref, o_ref):
    gates = _sigmoid(gate_ref[...].astype(F32))
    mixed = (gates[:, :D_MODEL] * _dot(att_ref[...], wpa_ref[...])
             + gates[:, D_MODEL:] * _dot(ssd_ref[...], wps_ref[...]))
    x1 = x_ref[...] + _dot(mixed.astype(BF16), wout_ref[...])
    h = _rms(x1, gc_ref[...]).astype(BF16)
    q = (_dot(h, wq_ref[...]) * CROSS_HEAD_DIM ** -0.5).astype(BF16)
    outs = []
    for hh in range(CROSS_HEADS):
        hs = slice(hh * CROSS_HEAD_DIM, (hh + 1) * CROSS_HEAD_DIM)
        vs = slice(D_MODEL + hh * CROSS_HEAD_DIM, D_MODEL + (hh + 1) * CROSS_HEAD_DIM)
        s = _dot_nt(q[:, hs], kv_ref[0, :, hs])
        e = jnp.exp(s - jnp.max(s, axis=-1, keepdims=True))
        den = jnp.sum(e, axis=-1, keepdims=True)
        outs.append(_dot(e.astype(BF16), kv_ref[0, :, vs]) * (1.0 / den))
    o = jnp.concatenate(outs, axis=-1).astype(BF16)
    o_ref[...] = x1 + _dot(o, wo_ref[...])


def _merge_cross(ssd_o, attn_o, proj2d, x2d, kv, w, tm, seq):
    m = x2d.shape[0]
    tiles_per_seq = seq // tm
    return pl.pallas_call(
        _merge_cross_kernel,
        grid=(m // tm,),
        in_specs=[pl.BlockSpec((tm, D_INNER), lambda i: (i, 0)),
                  pl.BlockSpec((tm, ATTN_WIDTH), lambda i: (i, 0)),
                  pl.BlockSpec((tm, 2 * D_MODEL), lambda i: (i, COL_GATES)),
                  pl.BlockSpec((tm, D_MODEL), lambda i: (i, 0)),
                  pl.BlockSpec((1, MEM_LEN, 2 * D_MODEL), lambda i: (i // tiles_per_seq, 0, 0)),
                  _resident((ATTN_WIDTH, D_MODEL)), _resident((D_INNER, D_MODEL)),
                  _resident((D_MODEL, D_MODEL)), _resident((1, D_MODEL)),
                  _resident((D_MODEL, D_MODEL)), _resident((D_MODEL, D_MODEL))],
        out_specs=pl.BlockSpec((tm, D_MODEL), lambda i: (i, 0)),
        out_shape=jax.ShapeDtypeStruct((m, D_MODEL), F32),
        compiler_params=_params(1),
        name="merge_cross",
    )(ssd_o, attn_o, proj2d, x2d, kv, w["w_proj_attn"], w["w_proj_ssd"], w["w_out"],
      w["norm_cross"], w["w_q_cross"], w["w_o_cross"])


def _ffn_chunks():
    chunks, f0 = [], 0
    while f0 < D_FF:
        fw = min(MATMUL_N_CHUNK, D_FF - f0)
        chunks.append((f0, fw))
        f0 += fw
    return chunks


def _ffn_kernel(final_norm, x_ref, g_ref, wgu_ref, wd_ref, gf_ref, o_ref, acc_ref):
    x = x_ref[...]
    h = _rms(x, g_ref[...]).astype(BF16)
    for n, (f0, fw) in enumerate(_ffn_chunks()):
        gate = _dot(h, wgu_ref[:, f0:f0 + fw])
        up = _dot(h, wgu_ref[:, D_FF + f0:D_FF + f0 + fw])
        part = _dot((_silu(gate) * up).astype(BF16), wd_ref[f0:f0 + fw, :])
        if n == 0:
            acc_ref[...] = x + part
        else:
            acc_ref[...] += part
    y = acc_ref[...]
    o_ref[...] = _rms(y, gf_ref[...]) if final_norm else y


def _ffn(x2d, g, w_gate_up, w_down, g_final, final_norm, tm):
    m = x2d.shape[0]
    return pl.pallas_call(
        functools.partial(_ffn_kernel, final_norm),
        grid=(m // tm,),
        in_specs=[pl.BlockSpec((tm, D_MODEL), lambda i: (i, 0)),
                  _resident((1, D_MODEL)),
                  _resident((D_MODEL, 2 * D_FF)), _resident((D_FF, D_MODEL)),
                  _resident((1, D_MODEL))],
        out_specs=pl.BlockSpec((tm, D_MODEL), lambda i: (i, 0)),
        out_shape=jax.ShapeDtypeStruct((m, D_MODEL), F32),
        scratch_shapes=[pltpu.VMEM((tm, D_MODEL), F32)],
        compiler_params=_params(1),
        name="ffn_final" if final_norm else "ffn",
    )(x2d, g, w_gate_up, w_down, g_final)


def _t5_bucket(rel):
    nb = NUM_BUCKETS // 2
    max_exact = nb // 2
    ret = jnp.where(rel > 0, nb, 0)
    n = jnp.abs(rel)
    nf = jnp.maximum(n, 1).astype(F32)
    large = max_exact + (jnp.log(nf / max_exact) / math.log(MAX_DISTANCE / max_exact)
                         * (nb - max_exact)).astype(jnp.int32)
    large = jnp.minimum(large, nb - 1)
    return ret + jnp.where(n < max_exact, n, large)


def _attention_bias(rel_bias):
    rel = jnp.arange(SPAN)[None, :] - WINDOW - jnp.arange(BLOCK)[:, None]
    one_hot = (_t5_bucket(rel)[None] == jnp.arange(NUM_BUCKETS)[:, None, None]).astype(F32)
    bias = jnp.einsum("bh,bqk->hqk", rel_bias.astype(F32), one_hot, precision=lax.Precision.HIGHEST)
    bias = jnp.where((jnp.abs(rel) <= WINDOW)[None], bias, NEG_BIG)
    col = jnp.arange(SPAN)
    variants = []
    for first, last in ((False, False), (True, False), (False, True), (True, True)):
        pad = (first & (col < WINDOW)) | (last & (col >= WINDOW + BLOCK))
        masked = jnp.where(pad[None, None, :], NEG_BIG, bias).reshape(N_KV_HEADS, Q_PER_KV, BLOCK, SPAN)
        variants.append(jnp.transpose(masked, (0, 3, 1, 2)).reshape(N_KV_HEADS * SPAN, Q_PER_KV * BLOCK))
    return jnp.stack(variants)


def _head_expand_matrix(lane0):
    e = np.zeros((2 * DT_PAD, D_INNER), np.float32)
    for h in range(SSD_HEADS):
        e[lane0 + h, h * SSD_HEAD_DIM:(h + 1) * SSD_HEAD_DIM] = 1.0
        e[DT_PAD + lane0 + h, h * SSD_HEAD_DIM:(h + 1) * SSD_HEAD_DIM] = 1.0
    return jnp.asarray(e, BF16)


def _tables(rel_bias):
    return {"attn_bias": _attention_bias(rel_bias), "expand_fwd": _head_expand_matrix(0),
            "expand_bwd": _head_expand_matrix(SSD_HEADS)}


def _prepare_layer(l, p):
    w_in = p["w_in"][l]
    o_q, o_k, o_v = 0, ATTN_WIDTH, ATTN_WIDTH + KV_WIDTH
    o_z = ATTN_WIDTH + 2 * KV_WIDTH
    o_xs = o_z + D_INNER
    o_bc = o_xs + D_INNER
    o_dt = o_bc + BC_WIDTH
    o_g = o_dt + 2 * SSD_HEADS
    w_main = jnp.concatenate([w_in[:, o_z:o_xs], w_in[:, o_xs:o_bc], w_in[:, o_g:o_g + 2 * D_MODEL],
                              w_in[:, o_q:o_k], w_in[:, o_bc:o_dt], w_in[:, o_k:o_v],
                              w_in[:, o_v:o_z]], axis=1).astype(BF16)
    w_dt = jnp.pad(w_in[:, o_dt:o_g], ((0, 0), (0, DT_PAD - 2 * SSD_HEADS))).astype(BF16)
    pad_dt = lambda a: jnp.pad(a.reshape(1, 2 * SSD_HEADS), ((0, 0), (0, DT_PAD - 2 * SSD_HEADS)))
    row = lambda a: a.reshape(1, -1).astype(F32)
    return {
        "norm_mix": row(p["norm_mix"][l]), "w_main": w_main, "w_dt": w_dt,
        "attn_sink": jnp.repeat(p["attn_sink"][l].astype(F32), BLOCK).reshape(1, N_HEADS * BLOCK),
        "conv_w": p["conv_w"][l].astype(F32), "conv_b": row(p["conv_b"][l]),
        "dt_bias": pad_dt(p["dt_bias"][l]), "a_log": pad_dt(p["a_log"][l]),
        "d_skip": row(jnp.repeat(p["d_skip"][l], SSD_HEAD_DIM)),
        "norm_ssd": row(p["norm_ssd"][l]),
        "w_proj_attn": p["w_proj_attn"][l].astype(BF16), "w_proj_ssd": p["w_proj_ssd"][l].astype(BF16),
        "w_out": p["w_out"][l].astype(BF16), "norm_cross": row(p["norm_cross"][l]),
        "norm_mem": row(p["norm_mem"][l]), "w_q_cross": p["w_q_cross"][l].astype(BF16),
        "w_kv_cross": p["w_kv_cross"][l].astype(BF16), "w_o_cross": p["w_o_cross"][l].astype(BF16),
        "norm_ffn": row(p["norm_ffn"][l]), "w_gate_up": p["w_gate_up"][l].astype(BF16),
        "w_down": p["w_down"][l].astype(BF16),
    }


def _encoder(x, mem, layers, tables, norm_final):
    b, s, _ = x.shape
    m = b * s
    tm = min(ROW_TILE, s)
    x2d = x.reshape(m, D_MODEL)
    for l, w in enumerate(layers):
        proj, dt = _in_proj(x2d, w, tm, s)
        proj3d = proj.reshape(b, s, PROJ_WIDTH)
        attn_o, ssd_o = _mixer_calls(proj3d, dt.reshape(b, s, DT_PAD), tables, w)
        kv = _mem_kv(mem, w["norm_mem"], w["w_kv_cross"])
        x2d = _merge_cross(ssd_o.reshape(m, D_INNER), attn_o.reshape(m, ATTN_WIDTH), proj, x2d, kv, w,
                           tm, s)
        x2d = _ffn(x2d, w["norm_ffn"], w["w_gate_up"], w["w_down"], norm_final,
                   l == len(layers) - 1, tm)
    return x2d.reshape(b, s, D_MODEL)


def kernel(x_prompt, x_sample, mem_prompt, mem_sample, rel_bias, norm_mix, w_in, attn_sink, conv_w,
           conv_b, dt_bias, a_log, d_skip, norm_ssd, w_proj_attn, w_proj_ssd, w_out, norm_cross,
           norm_mem, w_q_cross, w_kv_cross, w_o_cross, norm_ffn, w_gate_up, w_down, norm_final):
    p = dict(norm_mix=norm_mix, w_in=w_in, attn_sink=attn_sink, conv_w=conv_w, conv_b=conv_b,
             dt_bias=dt_bias, a_log=a_log, d_skip=d_skip, norm_ssd=norm_ssd, w_proj_attn=w_proj_attn,
             w_proj_ssd=w_proj_ssd, w_out=w_out, norm_cross=norm_cross, norm_mem=norm_mem,
             w_q_cross=w_q_cross, w_kv_cross=w_kv_cross, w_o_cross=w_o_cross, norm_ffn=norm_ffn,
             w_gate_up=w_gate_up, w_down=w_down)
    layers = [_prepare_layer(l, p) for l in range(norm_mix.shape[0])]
    tables = _tables(rel_bias)
    g_final = norm_final.reshape(1, D_MODEL).astype(F32)
    y_prompt = _encoder(x_prompt, mem_prompt, layers, tables, g_final)
    y_sample = _encoder(x_sample, mem_sample, layers, tables, g_final)
    return (y_prompt, y_sample)
```

```python
import functools
import itertools
import math

import numpy as np
import jax
import jax.numpy as jnp
from jax import lax
from jax.experimental import pallas as pl
from jax.experimental.pallas import tpu as pltpu

F32 = jnp.float32
BF16 = jnp.bfloat16

D_MODEL = 1024
N_HEADS = 16
N_KV_HEADS = 4
HEAD_DIM = 64
Q_PER_KV = N_HEADS // N_KV_HEADS
ATTN_WIDTH = N_HEADS * HEAD_DIM
KV_WIDTH = N_KV_HEADS * HEAD_DIM
WINDOW = 128
BLOCK = 128
SPAN = BLOCK + 2 * WINDOW
NUM_BUCKETS = 32
MAX_DISTANCE = 128
D_INNER = 2 * D_MODEL
SSD_HEAD_DIM = 64
SSD_HEADS = D_INNER // SSD_HEAD_DIM
SSD_GROUPS = 4
HEADS_PER_GROUP = SSD_HEADS // SSD_GROUPS
GROUP_WIDTH = HEADS_PER_GROUP * SSD_HEAD_DIM
D_STATE = 128
BC_WIDTH = 2 * SSD_GROUPS * D_STATE
CONV_K = 5
CONV_PAD = CONV_K // 2
CHUNK = 128
MEM_LEN = 256
CROSS_HEADS = 4
CROSS_HEAD_DIM = D_MODEL // CROSS_HEADS
D_FF = ((8 * D_MODEL // 3 + 255) // 256) * 256
EPS = 1e-6
LOG2_E = math.log2(math.e)

PROJ_WIDTH = 2 * D_INNER + 2 * D_MODEL + ATTN_WIDTH + BC_WIDTH + 2 * KV_WIDTH
COL_Z = 0
COL_XS = 1
COL_GATES = 2
COL_Q = (3 * D_INNER) // ATTN_WIDTH
COL_BC = (3 * D_INNER + ATTN_WIDTH) // BC_WIDTH
COL_K = (3 * D_INNER + ATTN_WIDTH + BC_WIDTH) // KV_WIDTH
COL_V = COL_K + 1
DT_PAD = 128

LANES = 128
SUBLANES = 8
HALO_ROWS = 16
CONV_ROWS = 64
SSD_CHUNKS_PER_STEP = 4
ATTN_BLOCKS_PER_STEP = 4
MATMUL_N_CHUNK = 256
IN_PROJ_N_CHUNK = 256
ROW_TILE = 512
NEG_BIG = -1e30
VMEM_LIMIT = 56 * 1024 * 1024


def _rms(x, g):
    return x * lax.rsqrt(jnp.mean(x * x, axis=-1, keepdims=True) + EPS) * g


def _silu(x):
    return x * lax.logistic(x)


def _sigmoid(x):
    return lax.logistic(x)


def _softplus(x):
    return jnp.maximum(x, 0.0) + jnp.log1p(jnp.exp(-jnp.abs(x)))


def _dot(a, b):
    return jnp.dot(a, b, preferred_element_type=F32)


def _dot_nt(a, b):
    return lax.dot_general(a, b, (((1,), (1,)), ((), ())), preferred_element_type=F32)


def _resident(shape):
    nd = len(shape)
    return pl.BlockSpec(shape, lambda *_: (0,) * nd, pipeline_mode=pl.Buffered(1))


def _params(n_axes):
    return pltpu.CompilerParams(dimension_semantics=("arbitrary",) * n_axes,
                                vmem_limit_bytes=VMEM_LIMIT)


def _in_proj_kernel(tiles_per_seq, x_ref, xp_ref, xn_ref, g_ref, w_ref, wdt_ref, cw_ref, cb_ref,
                    o_ref, dt_ref, h_ext, ext):
    i = pl.program_id(0)
    tm = x_ref.shape[0]
    g = g_ref[...]
    h_ext[0:HALO_ROWS, :] = jnp.where(i % tiles_per_seq != 0, _rms(xp_ref[...], g), 0.0).astype(BF16)
    h_ext[HALO_ROWS:HALO_ROWS + tm, :] = _rms(x_ref[...], g).astype(BF16)
    h_ext[HALO_ROWS + tm:, :] = jnp.where((i + 1) % tiles_per_seq != 0, _rms(xn_ref[...], g), 0.0).astype(BF16)
    blk_rows = CONV_ROWS + 2 * SUBLANES

    def finish(n0, main):
        conv_col = _conv_column(n0)
        if conv_col is None:
            if n0 < (COL_Z + 1) * D_INNER:
                main = _silu(main)
            o_ref[:, n0:n0 + IN_PROJ_N_CHUNK] = main.astype(BF16)
            return
        for r0 in range(0, tm, CONV_ROWS):
            for l0 in range(0, IN_PROJ_N_CHUNK, LANES):
                cc = slice(conv_col + l0, conv_col + l0 + LANES)
                blk = ext[HALO_ROWS - SUBLANES + r0:HALO_ROWS - SUBLANES + r0 + blk_rows, l0:l0 + LANES]
                acc = cb_ref[:, cc] + cw_ref[CONV_PAD:CONV_PAD + 1, cc] * blk[SUBLANES:SUBLANES + CONV_ROWS]
                for t in range(CONV_K):
                    if t != CONV_PAD:
                        shifted = pltpu.roll(blk, (CONV_PAD - t) % blk_rows, axis=0)
                        acc = acc + cw_ref[t:t + 1, cc] * shifted[SUBLANES:SUBLANES + CONV_ROWS]
                o_ref[r0:r0 + CONV_ROWS, n0 + l0:n0 + l0 + LANES] = _silu(acc).astype(BF16)

    pending = None
    for n0 in _chunk_order():
        cols = slice(n0, n0 + IN_PROJ_N_CHUNK)
        if _conv_column(n0) is None:
            main = _dot(h_ext[HALO_ROWS:HALO_ROWS + tm, :], w_ref[:, cols])
        else:
            ext[...] = _dot(h_ext[...], w_ref[:, cols])
            main = None
        if pending is not None:
            finish(*pending)
        pending = (n0, main)
    dt_ref[...] = _dot(h_ext[HALO_ROWS:HALO_ROWS + tm, :], wdt_ref[...])
    finish(*pending)


def _chunk_order():
    chunks = list(range(0, PROJ_WIDTH, IN_PROJ_N_CHUNK))
    heavy = [n0 for n0 in chunks if _conv_column(n0) is not None]
    light = [n0 for n0 in chunks if _conv_column(n0) is None]
    order = []
    while heavy or light:
        if heavy:
            order.append(heavy.pop(0))
        take = -(-len(light) // (len(heavy) + 1))
        order.extend(light[:take])
        light = light[take:]
    return order


def _conv_column(n0):
    if COL_XS * D_INNER <= n0 < (COL_XS + 1) * D_INNER:
        return n0 - COL_XS * D_INNER
    if COL_BC * BC_WIDTH <= n0 < (COL_BC + 1) * BC_WIDTH:
        return D_INNER + n0 - COL_BC * BC_WIDTH
    return None


def _in_proj(x2d, w, tm, seq):
    m = x2d.shape[0]
    halo_per_tile = tm // HALO_ROWS
    n_halo = m // HALO_ROWS
    return pl.pallas_call(
        functools.partial(_in_proj_kernel, seq // tm),
        grid=(m // tm,),
        in_specs=[pl.BlockSpec((tm, D_MODEL), lambda i: (i, 0)),
                  pl.BlockSpec((HALO_ROWS, D_MODEL), lambda i: (jnp.maximum(i * halo_per_tile - 1, 0), 0)),
                  pl.BlockSpec((HALO_ROWS, D_MODEL),
                               lambda i: (jnp.minimum((i + 1) * halo_per_tile, n_halo - 1), 0)),
                  _resident((1, D_MODEL)),
                  _resident((D_MODEL, PROJ_WIDTH)),
                  _resident((D_MODEL, DT_PAD)),
                  _resident((CONV_K, D_INNER + BC_WIDTH)), _resident((1, D_INNER + BC_WIDTH))],
        out_specs=[pl.BlockSpec((tm, PROJ_WIDTH), lambda i: (i, 0)),
                   pl.BlockSpec((tm, DT_PAD), lambda i: (i, 0))],
        out_shape=[jax.ShapeDtypeStruct((m, PROJ_WIDTH), BF16),
                   jax.ShapeDtypeStruct((m, DT_PAD), F32)],
        scratch_shapes=[pltpu.VMEM((tm + 2 * HALO_ROWS, D_MODEL), BF16),
                        pltpu.VMEM((tm + 2 * HALO_ROWS, IN_PROJ_N_CHUNK), F32)],
        compiler_params=_params(1),
        name="in_proj",
    )(x2d, x2d, x2d, w["norm_mix"], w["w_main"], w["w_dt"], w["conv_w"], w["conv_b"])


def _attn_kernel(n_sub, *refs):
    for _ in _attn_steps(n_sub, *refs):
        pass


def _attn_steps(n_sub, q_ref, kp_ref, kc_ref, kn_ref, vp_ref, vc_ref, vn_ref, sink_ref, *refs):
    bias_refs, o_ref = refs[:n_sub], refs[n_sub]
    k_all = jnp.concatenate([kp_ref[0], kc_ref[0], kn_ref[0]], axis=0)
    v_all = jnp.concatenate([vp_ref[0], vc_ref[0], vn_ref[0]], axis=0)
    vt_all = v_all.astype(F32).T.astype(BF16)
    cols = Q_PER_KV * BLOCK
    ones_rows = (lax.broadcasted_iota(jnp.int32, (HALO_ROWS, SPAN), 0) == 0).astype(BF16)
    for sub in range(n_sub):
        q = q_ref[0, sub * BLOCK:(sub + 1) * BLOCK, :] * jnp.asarray(HEAD_DIM ** -0.5, BF16)
        k = k_all[sub * BLOCK:sub * BLOCK + SPAN]
        v_t = vt_all[:, sub * BLOCK:sub * BLOCK + SPAN]
        bias_ref = bias_refs[sub]

        def logits(g):
            q4 = jnp.concatenate([q[:, (g * Q_PER_KV + r) * HEAD_DIM:(g * Q_PER_KV + r + 1) * HEAD_DIM]
                                  for r in range(Q_PER_KV)], axis=0)
            return (_dot_nt(k[:, g * HEAD_DIM:(g + 1) * HEAD_DIM], q4)
                    + bias_ref[0, g * SPAN:(g + 1) * SPAN, :])

        outs = []
        s_next = logits(0)
        for g in range(N_KV_HEADS):
            s = s_next
            if g + 1 < N_KV_HEADS:
                s_next = logits(g + 1)
            sink = sink_ref[:, g * cols:(g + 1) * cols]
            m = jnp.maximum(jnp.max(s, axis=0, keepdims=True), sink)
            e = jnp.exp(s - m).astype(BF16)
            v_aug = jnp.concatenate([v_t[g * HEAD_DIM:(g + 1) * HEAD_DIM], ones_rows], axis=0)
            o_aug = _dot(v_aug, e)
            den = o_aug[HEAD_DIM:HEAD_DIM + 1] + jnp.exp(sink - m)
            o_t = o_aug[:HEAD_DIM] * (1.0 / den)
            outs.extend(o_t[:, r * BLOCK:(r + 1) * BLOCK] for r in range(Q_PER_KV))
            if g + 1 < N_KV_HEADS:
                yield
        o_ref[0, sub * BLOCK:(sub + 1) * BLOCK, :] = jnp.concatenate(outs, axis=0).T.astype(BF16)
        yield


def _window_attention(proj3d, bias, sink):
    b, s, _ = proj3d.shape
    nblk = s // BLOCK
    n_sub = ATTN_BLOCKS_PER_STEP if nblk % ATTN_BLOCKS_PER_STEP == 0 else 1
    nstep = nblk // n_sub
    in_specs, out_spec = _attention_specs(nblk, n_sub, lambda i: i)
    return pl.pallas_call(
        functools.partial(_attn_kernel, n_sub),
        grid=(b, nstep),
        in_specs=in_specs,
        out_specs=out_spec,
        out_shape=jax.ShapeDtypeStruct((b, s, ATTN_WIDTH), BF16),
        compiler_params=_params(2),
        name="window_attention",
    )(*_attention_operands(proj3d, bias, sink, n_sub))


def _attention_specs(nblk, n_sub, pos):
    def edge_spec(col, index):
        return pl.BlockSpec((1, BLOCK, KV_WIDTH), lambda bi, i: (bi, index(pos(i)), col))

    def kv_specs(col):
        return [edge_spec(col, lambda p: jnp.maximum(p * n_sub - 1, 0)),
                pl.BlockSpec((1, n_sub * BLOCK, KV_WIDTH), lambda bi, i: (bi, pos(i), col)),
                edge_spec(col, lambda p: jnp.minimum((p + 1) * n_sub, nblk - 1))]

    def bias_spec(sub):
        def variant(bi, i):
            first = (pos(i) * n_sub + sub == 0).astype(jnp.int32)
            last = (pos(i) * n_sub + sub == nblk - 1).astype(jnp.int32)
            return (first + 2 * last, 0, 0)
        return pl.BlockSpec((1, N_KV_HEADS * SPAN, Q_PER_KV * BLOCK), variant)

    in_specs = [pl.BlockSpec((1, n_sub * BLOCK, ATTN_WIDTH), lambda bi, i: (bi, pos(i), COL_Q)),
                *kv_specs(COL_K), *kv_specs(COL_V), _resident((1, N_HEADS * BLOCK)),
                *[bias_spec(sub) for sub in range(n_sub)]]
    return in_specs, pl.BlockSpec((1, n_sub * BLOCK, ATTN_WIDTH), lambda bi, i: (bi, pos(i), 0))


def _attention_operands(proj3d, bias, sink, n_sub):
    return [proj3d] * 7 + [sink] + [bias] * n_sub


def _split_hi_lo(v):
    hi = v.astype(BF16)
    return hi, (v - hi.astype(F32)).astype(BF16)


def _scan_tables(forward, dt_raw, dtb_ref, alog_ref, e2_ref):
    total_row = CHUNK - 1 if forward else 0
    dtv = _softplus(dt_raw + dtb_ref[...])
    adt = dtv * (-jnp.exp(alog_ref[...])) * LOG2_E
    ii = lax.broadcasted_iota(jnp.int32, (CHUNK, CHUNK), 0)
    jj = lax.broadcasted_iota(jnp.int32, (CHUNK, CHUNK), 1)
    scanned = (jj <= ii) if forward else (jj >= ii)
    tri = scanned.astype(BF16)
    a1 = adt.astype(BF16)
    r1 = adt - a1.astype(F32)
    a2 = r1.astype(BF16)
    a3 = (r1 - a2.astype(F32)).astype(BF16)
    pc = _dot(jnp.concatenate([tri, tri, tri], axis=1), jnp.concatenate([a1, a2, a3], axis=0))
    total = pc[total_row:total_row + 1, :]
    both = jnp.concatenate([jnp.exp2(pc), jnp.exp2(total - pc) * dtv], axis=0)
    hi, lo = _split_hi_lo(both)
    both_e = _dot(jnp.concatenate([hi, lo], axis=1), e2_ref[...])
    off_e = both_e[:CHUNK]
    w_e = both_e[CHUNK:]
    decay = off_e[total_row:total_row + 1, :]
    row_t = (pc - jnp.log(dtv) * LOG2_E).T
    return scanned, pc, row_t, off_e, w_e, decay


def _scan_group(g, lane0, scanned, pc, row_t, off_e, decay, b_t, c_bf, x_bf, xw_bf, state):
    gcols = slice(g * GROUP_WIDTH, (g + 1) * GROUP_WIDTH)
    gmat = _dot(c_bf, b_t)
    sg = state[g]
    y_off = _dot(c_bf, sg.astype(BF16)) * off_e[:, gcols]

    def decay_block(r):
        lane = lane0 + g * HEADS_PER_GROUP + r
        delta = pc[:, lane:lane + 1] - row_t[lane:lane + 1, :]
        return (gmat * jnp.exp2(jnp.where(scanned, delta, NEG_BIG))).astype(BF16)

    first_head = lax.broadcasted_iota(jnp.int32, (1, LANES), 1) < SSD_HEAD_DIM
    ys = []
    for q in range(HEADS_PER_GROUP // 2):
        x_pair = x_bf[:, q * LANES:(q + 1) * LANES]
        zero = jnp.zeros_like(x_pair)
        rhs = jnp.concatenate([jnp.where(first_head, x_pair, zero), jnp.where(first_head, zero, x_pair)],
                              axis=0)
        ys.append(_dot(jnp.concatenate([decay_block(2 * q), decay_block(2 * q + 1)], axis=1), rhs))
    state[g] = sg * decay[:, gcols] + _dot(b_t, xw_bf)
    return jnp.concatenate(ys, axis=-1) + y_off


def _reset_state_at_row_start(state):
    @pl.when(pl.program_id(1) == 0)
    def _():
        state[...] = jnp.zeros_like(state)


def _ssd_bwd_kernel(*refs):
    _reset_state_at_row_start(refs[-1])
    for _ in _ssd_bwd_steps(*refs):
        pass


def _ssd_bwd_steps(x_ref, bc_ref, dt_ref, dtb_ref, alog_ref, e2_ref, y_ref, bt_ref, state):
    order = list(reversed(range(x_ref.shape[1] // CHUNK)))
    tabs = {ci: _scan_tables(False, dt_ref[0, ci * CHUNK:(ci + 1) * CHUNK, :], dtb_ref, alog_ref, e2_ref)
            for ci in order}
    bc_half = BC_WIDTH // 2
    for ci in order:
        rows = slice(ci * CHUNK, (ci + 1) * CHUNK)
        scanned, pc, row_t, off_e, w_e, decay = tabs[ci]
        for g in range(SSD_GROUPS):
            gcols = slice(g * GROUP_WIDTH, (g + 1) * GROUP_WIDTH)
            x_bf = x_ref[0, rows, gcols]
            b_t = bc_ref[0, rows, g * D_STATE:(g + 1) * D_STATE].astype(F32).T.astype(BF16)
            bt_ref[0, ci * bc_half + g * D_STATE:ci * bc_half + (g + 1) * D_STATE, :] = b_t
            c_bf = bc_ref[0, rows, (SSD_GROUPS + g) * D_STATE:(SSD_GROUPS + g + 1) * D_STATE]
            y = _scan_group(g, SSD_HEADS, scanned, pc, row_t, off_e, decay, b_t, c_bf, x_bf,
                            x_bf * w_e[:, gcols].astype(BF16), state)
            y_ref[0, rows, gcols] = y.astype(BF16)
            yield


def _ssd_fwd_kernel(x_ref, c_ref, bt_ref, dt_ref, dtb_ref, alog_ref, e2_ref, yb_ref, z_ref, dskip_ref,
                    gn_ref, out_ref, state, ybuf):
    @pl.when(pl.program_id(1) == 0)
    def _():
        state[...] = jnp.zeros_like(state)

    order = list(range(x_ref.shape[1] // CHUNK))
    tabs = {ci: _scan_tables(True, dt_ref[0, ci * CHUNK:(ci + 1) * CHUNK, :], dtb_ref, alog_ref, e2_ref)
            for ci in order}
    bc_half = BC_WIDTH // 2
    for ci in order:
        rows = slice(ci * CHUNK, (ci + 1) * CHUNK)
        scanned, pc, row_t, off_e, w_e, decay = tabs[ci]
        for g in range(SSD_GROUPS):
            gcols = slice(g * GROUP_WIDTH, (g + 1) * GROUP_WIDTH)
            x_bf = x_ref[0, rows, gcols]
            xg = x_bf.astype(F32)
            b_t = bt_ref[0, ci * bc_half + g * D_STATE:ci * bc_half + (g + 1) * D_STATE, :]
            c_bf = c_ref[0, rows, g * D_STATE:(g + 1) * D_STATE]
            y = _scan_group(g, 0, scanned, pc, row_t, off_e, decay, b_t, c_bf, x_bf,
                            x_bf * w_e[:, gcols].astype(BF16), state)
            ybuf[rows, gcols] = y + yb_ref[0, rows, gcols].astype(F32) + xg * dskip_ref[:, gcols]
        gated = ybuf[rows, :] * z_ref[0, rows, :].astype(F32)
        out_ref[0, rows, :] = _rms(gated, gn_ref[...]).astype(BF16)


def _attn_ssd_bwd_kernel(n_sub, *refs):
    n_attn_in = 8 + n_sub
    n_ssd_in = 6
    o_ref, y_ref, bt_ref, state = refs[n_attn_in + n_ssd_in:]
    _reset_state_at_row_start(state)
    attn = _attn_steps(n_sub, *refs[:n_attn_in], o_ref)
    ssd = _ssd_bwd_steps(*refs[n_attn_in:n_attn_in + n_ssd_in], y_ref, bt_ref, state)
    for _ in itertools.zip_longest(attn, ssd):
        pass


def _mixer_calls(proj3d, dt3d, tables, w):
    b, s, _ = proj3d.shape
    nchunk = s // CHUNK
    per_step = SSD_CHUNKS_PER_STEP if nchunk % SSD_CHUNKS_PER_STEP == 0 else 1
    nstep = nchunk // per_step
    rows = per_step * CHUNK
    bc_half = BC_WIDTH // 2
    fused = per_step == ATTN_BLOCKS_PER_STEP and BLOCK == CHUNK

    def rev_spec(width, col=0):
        return pl.BlockSpec((1, rows, width), lambda bi, c: (bi, nstep - 1 - c, col))

    def chunk_spec(width, col=0):
        return pl.BlockSpec((1, rows, width), lambda bi, c: (bi, c, col))

    state = pltpu.VMEM((SSD_GROUPS, D_STATE, GROUP_WIDTH), F32)
    bwd_in_specs = [rev_spec(D_INNER, COL_XS), rev_spec(BC_WIDTH, COL_BC), rev_spec(DT_PAD),
                    _resident((1, DT_PAD)), _resident((1, DT_PAD)), _resident((2 * DT_PAD, D_INNER))]
    bwd_out_specs = [rev_spec(D_INNER),
                     pl.BlockSpec((1, per_step * bc_half, CHUNK), lambda bi, c: (bi, nstep - 1 - c, 0))]
    bwd_out_shape = [jax.ShapeDtypeStruct((b, s, D_INNER), BF16),
                     jax.ShapeDtypeStruct((b, nchunk * bc_half, CHUNK), BF16)]
    bwd_operands = [proj3d, proj3d, dt3d, w["dt_bias"], w["a_log"], tables["expand_bwd"]]
    if fused:
        attn_in_specs, attn_out_spec = _attention_specs(nchunk, per_step, lambda c: nstep - 1 - c)
        attn_o, y_bwd, bt = pl.pallas_call(
            functools.partial(_attn_ssd_bwd_kernel, per_step),
            grid=(b, nstep),
            in_specs=attn_in_specs + bwd_in_specs,
            out_specs=[attn_out_spec] + bwd_out_specs,
            out_shape=[jax.ShapeDtypeStruct((b, s, ATTN_WIDTH), BF16)] + bwd_out_shape,
            scratch_shapes=[state],
            compiler_params=_params(2),
            name="attention_ssd_backward",
        )(*_attention_operands(proj3d, tables["attn_bias"], w["attn_sink"], per_step), *bwd_operands)
    else:
        attn_o = _window_attention(proj3d, tables["attn_bias"], w["attn_sink"])
        y_bwd, bt = pl.pallas_call(
            _ssd_bwd_kernel,
            grid=(b, nstep),
            in_specs=bwd_in_specs,
            out_specs=bwd_out_specs,
            out_shape=bwd_out_shape,
            scratch_shapes=[state],
            compiler_params=_params(2),
            name="ssd_backward",
        )(*bwd_operands)

    ssd_o = pl.pallas_call(
        _ssd_fwd_kernel,
        grid=(b, nstep),
        in_specs=[chunk_spec(D_INNER, COL_XS), chunk_spec(bc_half, COL_BC * 2 + 1),
                  pl.BlockSpec((1, per_step * bc_half, CHUNK), lambda bi, c: (bi, c, 0)),
                  chunk_spec(DT_PAD),
                  _resident((1, DT_PAD)), _resident((1, DT_PAD)), _resident((2 * DT_PAD, D_INNER)),
                  chunk_spec(D_INNER), chunk_spec(D_INNER, COL_Z),
                  _resident((1, D_INNER)), _resident((1, D_INNER))],
        out_specs=chunk_spec(D_INNER),
        out_shape=jax.ShapeDtypeStruct((b, s, D_INNER), BF16),
        scratch_shapes=[state, pltpu.VMEM((rows, D_INNER), F32)],
        compiler_params=_params(2),
        name="ssd_forward",
    )(proj3d, proj3d, bt, dt3d, w["dt_bias"], w["a_log"], tables["expand_fwd"], y_bwd, proj3d,
      w["d_skip"], w["norm_ssd"])
    return attn_o, ssd_o


def _mem_kv_kernel(mem_ref, g_ref, w_ref, o_ref):
    h = _rms(mem_ref[0], g_ref[...]).astype(BF16)
    o_ref[0] = _dot(h, w_ref[...]).astype(BF16)


def _mem_kv(mem, g, w_kv):
    b = mem.shape[0]
    return pl.pallas_call(
        _mem_kv_kernel,
        grid=(b,),
        in_specs=[pl.BlockSpec((1, MEM_LEN, D_MODEL), lambda i: (i, 0, 0)),
                  _resident((1, D_MODEL)), _resident((D_MODEL, 2 * D_MODEL))],
        out_specs=pl.BlockSpec((1, MEM_LEN, 2 * D_MODEL), lambda i: (i, 0, 0)),
        out_shape=jax.ShapeDtypeStruct((b, MEM_LEN, 2 * D_MODEL), BF16),
        compiler_params=_params(1),
        name="memory_kv",
    )(mem, g, w_kv)


def _merge_cross_kernel(ssd_ref, att_ref, gate_ref, x_ref, kv_ref, wpa_ref, wps_ref, wout_ref,
                        gc_ref, wq_ref, wo_ref, o_ref):
    gates = _sigmoid(gate_ref[...].astype(F32))
    mixed = (gates[:, :D_MODEL] * _dot(att_ref[...], wpa_ref[...])
             + gates[:, D_MODEL:] * _dot(ssd_ref[...], wps_ref[...]))
    x1 = x_ref[...] + _dot(mixed.astype(BF16), wout_ref[...])
    h = _rms(x1, gc_ref[...]).astype(BF16)
    q = (_dot(h, wq_ref[...]) * CROSS_HEAD_DIM ** -0.5).astype(BF16)
    outs = []
    for hh in range(CROSS_HEADS):
        hs = slice(hh * CROSS_HEAD_DIM, (hh + 1) * CROSS_HEAD_DIM)
        vs = slice(D_MODEL + hh * CROSS_HEAD_DIM, D_MODEL + (hh + 1) * CROSS_HEAD_DIM)
        s = _dot_nt(q[:, hs], kv_ref[0, :, hs])
        e = jnp.exp(s - jnp.max(s, axis=-1, keepdims=True))
        den = jnp.sum(e, axis=-1, keepdims=True)
        outs.append(_dot(e.astype(BF16), kv_ref[0, :, vs]) * (1.0 / den))
    o = jnp.concatenate(outs, axis=-1).astype(BF16)
    o_ref[...] = x1 + _dot(o, wo_ref[...])


def _merge_cross(ssd_o, attn_o, proj2d, x2d, kv, w, tm, seq):
    m = x2d.shape[0]
    tiles_per_seq = seq // tm
    return pl.pallas_call(
        _merge_cross_kernel,
        grid=(m // tm,),
        in_specs=[pl.BlockSpec((tm, D_INNER), lambda i: (i, 0)),
                  pl.BlockSpec((tm, ATTN_WIDTH), lambda i: (i, 0)),
                  pl.BlockSpec((tm, 2 * D_MODEL), lambda i: (i, COL_GATES)),
                  pl.BlockSpec((tm, D_MODEL), lambda i: (i, 0)),
                  pl.BlockSpec((1, MEM_LEN, 2 * D_MODEL), lambda i: (i // tiles_per_seq, 0, 0)),
                  _resident((ATTN_WIDTH, D_MODEL)), _resident((D_INNER, D_MODEL)),
                  _resident((D_MODEL, D_MODEL)), _resident((1, D_MODEL)),
                  _resident((D_MODEL, D_MODEL)), _resident((D_MODEL, D_MODEL))],
        out_specs=pl.BlockSpec((tm, D_MODEL), lambda i: (i, 0)),
        out_shape=jax.ShapeDtypeStruct((m, D_MODEL), F32),
        compiler_params=_params(1),
        name="merge_cross",
    )(ssd_o, attn_o, proj2d, x2d, kv, w["w_proj_attn"], w["w_proj_ssd"], w["w_out"],
      w["norm_cross"], w["w_q_cross"], w["w_o_cross"])


def _ffn_chunks():
    chunks, f0 = [], 0
    while f0 < D_FF:
        fw = min(MATMUL_N_CHUNK, D_FF - f0)
        chunks.append((f0, fw))
        f0 += fw
    return chunks


def _ffn_kernel(final_norm, x_ref, g_ref, wgu_ref, wd_ref, gf_ref, o_ref, acc_ref):
    x = x_ref[...]
    h = _rms(x, g_ref[...]).astype(BF16)
    for n, (f0, fw) in enumerate(_ffn_chunks()):
        gate = _dot(h, wgu_ref[:, f0:f0 + fw])
        up = _dot(h, wgu_ref[:, D_FF + f0:D_FF + f0 + fw])
        part = _dot((_silu(gate) * up).astype(BF16), wd_ref[f0:f0 + fw, :])
        if n == 0:
            acc_ref[...] = x + part
        else:
            acc_ref[...] += part
    y = acc_ref[...]
    o_ref[...] = _rms(y, gf_ref[...]) if final_norm else y


def _ffn(x2d, g, w_gate_up, w_down, g_final, final_norm, tm):
    m = x2d.shape[0]
    return pl.pallas_call(
        functools.partial(_ffn_kernel, final_norm),
        grid=(m // tm,),
        in_specs=[pl.BlockSpec((tm, D_MODEL), lambda i: (i, 0)),
                  _resident((1, D_MODEL)),
                  _resident((D_MODEL, 2 * D_FF)), _resident((D_FF, D_MODEL)),
                  _resident((1, D_MODEL))],
        out_specs=pl.BlockSpec((tm, D_MODEL), lambda i: (i, 0)),
        out_shape=jax.ShapeDtypeStruct((m, D_MODEL), F32),
        scratch_shapes=[pltpu.VMEM((tm, D_MODEL), F32)],
        compiler_params=_params(1),
        name="ffn_final" if final_norm else "ffn",
    )(x2d, g, w_gate_up, w_down, g_final)


def _t5_bucket(rel):
    nb = NUM_BUCKETS // 2
    max_exact = nb // 2
    ret = jnp.where(rel > 0, nb, 0)
    n = jnp.abs(rel)
    nf = jnp.maximum(n, 1).astype(F32)
    large = max_exact + (jnp.log(nf / max_exact) / math.log(MAX_DISTANCE / max_exact)
                         * (nb - max_exact)).astype(jnp.int32)
    large = jnp.minimum(large, nb - 1)
    return ret + jnp.where(n < max_exact, n, large)


def _attention_bias(rel_bias):
    rel = jnp.arange(SPAN)[None, :] - WINDOW - jnp.arange(BLOCK)[:, None]
    one_hot = (_t5_bucket(rel)[None] == jnp.arange(NUM_BUCKETS)[:, None, None]).astype(F32)
    bias = jnp.einsum("bh,bqk->hqk", rel_bias.astype(F32), one_hot, precision=lax.Precision.HIGHEST)
    bias = jnp.where((jnp.abs(rel) <= WINDOW)[None], bias, NEG_BIG)
    col = jnp.arange(SPAN)
    variants = []
    for first, last in ((False, False), (True, False), (False, True), (True, True)):
        pad = (first & (col < WINDOW)) | (last & (col >= WINDOW + BLOCK))
        masked = jnp.where(pad[None, None, :], NEG_BIG, bias).reshape(N_KV_HEADS, Q_PER_KV, BLOCK, SPAN)
        variants.append(jnp.transpose(masked, (0, 3, 1, 2)).reshape(N_KV_HEADS * SPAN, Q_PER_KV * BLOCK))
    return jnp.stack(variants)


def _head_expand_matrix(lane0):
    e = np.zeros((2 * DT_PAD, D_INNER), np.float32)
    for h in range(SSD_HEADS):
        e[lane0 + h, h * SSD_HEAD_DIM:(h + 1) * SSD_HEAD_DIM] = 1.0
        e[DT_PAD + lane0 + h, h * SSD_HEAD_DIM:(h + 1) * SSD_HEAD_DIM] = 1.0
    return jnp.asarray(e, BF16)


def _tables(rel_bias):
    return {"attn_bias": _attention_bias(rel_bias), "expand_fwd": _head_expand_matrix(0),
            "expand_bwd": _head_expand_matrix(SSD_HEADS)}


def _prepare_layer(l, p):
    w_in = p["w_in"][l]
    o_q, o_k, o_v = 0, ATTN_WIDTH, ATTN_WIDTH + KV_WIDTH
    o_z = ATTN_WIDTH + 2 * KV_WIDTH
    o_xs = o_z + D_INNER
    o_bc = o_xs + D_INNER
    o_dt = o_bc + BC_WIDTH
    o_g = o_dt + 2 * SSD_HEADS
    w_main = jnp.concatenate([w_in[:, o_z:o_xs], w_in[:, o_xs:o_bc], w_in[:, o_g:o_g + 2 * D_MODEL],
                              w_in[:, o_q:o_k], w_in[:, o_bc:o_dt], w_in[:, o_k:o_v],
                              w_in[:, o_v:o_z]], axis=1).astype(BF16)
    w_dt = jnp.pad(w_in[:, o_dt:o_g], ((0, 0), (0, DT_PAD - 2 * SSD_HEADS))).astype(BF16)
    pad_dt = lambda a: jnp.pad(a.reshape(1, 2 * SSD_HEADS), ((0, 0), (0, DT_PAD - 2 * SSD_HEADS)))
    row = lambda a: a.reshape(1, -1).astype(F32)
    return {
        "norm_mix": row(p["norm_mix"][l]), "w_main": w_main, "w_dt": w_dt,
        "attn_sink": jnp.repeat(p["attn_sink"][l].astype(F32), BLOCK).reshape(1, N_HEADS * BLOCK),
        "conv_w": p["conv_w"][l].astype(F32), "conv_b": row(p["conv_b"][l]),
        "dt_bias": pad_dt(p["dt_bias"][l]), "a_log": pad_dt(p["a_log"][l]),
        "d_skip": row(jnp.repeat(p["d_skip"][l], SSD_HEAD_DIM)),
        "norm_ssd": row(p["norm_ssd"][l]),
        "w_proj_attn": p["w_proj_attn"][l].astype(BF16), "w_proj_ssd": p["w_proj_ssd"][l].astype(BF16),
        "w_out": p["w_out"][l].astype(BF16), "norm_cross": row(p["norm_cross"][l]),
        "norm_mem": row(p["norm_mem"][l]), "w_q_cross": p["w_q_cross"][l].astype(BF16),
        "w_kv_cross": p["w_kv_cross"][l].astype(BF16), "w_o_cross": p["w_o_cross"][l].astype(BF16),
        "norm_ffn": row(p["norm_ffn"][l]), "w_gate_up": p["w_gate_up"][l].astype(BF16),
        "w_down": p["w_down"][l].astype(BF16),
    }


def _encoder(x, mem, layers, tables, norm_final):
    b, s, _ = x.shape
    m = b * s
    tm = min(ROW_TILE, s)
    x2d = x.reshape(m, D_MODEL)
    for l, w in enumerate(layers):
        proj, dt = _in_proj(x2d, w, tm, s)
        proj3d = proj.reshape(b, s, PROJ_WIDTH)
        attn_o, ssd_o = _mixer_calls(proj3d, dt.reshape(b, s, DT_PAD), tables, w)
        kv = _mem_kv(mem, w["norm_mem"], w["w_kv_cross"])
        x2d = _merge_cross(ssd_o.reshape(m, D_INNER), attn_o.reshape(m, ATTN_WIDTH), proj, x2d, kv, w,
                           tm, s)
        x2d = _ffn(x2d, w["norm_ffn"], w["w_gate_up"], w["w_down"], norm_final,
                   l == len(layers) - 1, tm)
    return x2d.reshape(b, s, D_MODEL)


def kernel(x_prompt, x_sample, mem_prompt, mem_sample, rel_bias, norm_mix, w_in, attn_sink, conv_w,
           conv_b, dt_bias, a_log, d_skip, norm_ssd, w_proj_attn, w_proj_ssd, w_out, norm_cross,
           norm_mem, w_q_cross, w_kv_cross, w_o_cross, norm_ffn, w_gate_up, w_down, norm_final):
    p = dict(norm_mix=norm_mix, w_in=w_in, attn_sink=attn_sink, conv_w=conv_w, conv_b=conv_b,
             dt_bias=dt_bias, a_log=a_log, d_skip=d_skip, norm_ssd=norm_ssd, w_proj_attn=w_proj_attn,
             w_proj_ssd=w_proj_ssd, w_out=w_out, norm_cross=norm_cross, norm_mem=norm_mem,
             w_q_cross=w_q_cross, w_kv_cross=w_kv_cross, w_o_cross=w_o_cross, norm_ffn=norm_ffn,
             w_gate_up=w_gate_up, w_down=w_down)
    layers = [_prepare_layer(l, p) for l in range(norm_mix.shape[0])]
    tables = _tables(rel_bias)
    g_final = norm_final.reshape(1, D_MODEL).astype(F32)
    y_prompt = _encoder(x_prompt, mem_prompt, layers, tables, g_final)
    y_sample = _encoder(x_sample, mem_sample, layers, tables, g_final)
    return (y_prompt, y_sample)
```

```python
import functools
import itertools
import math

import numpy as np
import jax
import jax.numpy as jnp
from jax import lax
from jax.experimental import pallas as pl
from jax.experimental.pallas import tpu as pltpu

F32 = jnp.float32
BF16 = jnp.bfloat16

D_MODEL = 1024
N_HEADS = 16
N_KV_HEADS = 4
HEAD_DIM = 64
Q_PER_KV = N_HEADS // N_KV_HEADS
ATTN_WIDTH = N_HEADS * HEAD_DIM
KV_WIDTH = N_KV_HEADS * HEAD_DIM
WINDOW = 128
BLOCK = 128
SPAN = BLOCK + 2 * WINDOW
NUM_BUCKETS = 32
MAX_DISTANCE = 128
D_INNER = 2 * D_MODEL
SSD_HEAD_DIM = 64
SSD_HEADS = D_INNER // SSD_HEAD_DIM
SSD_GROUPS = 4
HEADS_PER_GROUP = SSD_HEADS // SSD_GROUPS
GROUP_WIDTH = HEADS_PER_GROUP * SSD_HEAD_DIM
D_STATE = 128
BC_WIDTH = 2 * SSD_GROUPS * D_STATE
CONV_K = 5
CONV_PAD = CONV_K // 2
CHUNK = 128
MEM_LEN = 256
CROSS_HEADS = 4
CROSS_HEAD_DIM = D_MODEL // CROSS_HEADS
D_FF = ((8 * D_MODEL // 3 + 255) // 256) * 256
EPS = 1e-6
LOG2_E = math.log2(math.e)

PROJ_WIDTH = 2 * D_INNER + 2 * D_MODEL + ATTN_WIDTH + BC_WIDTH + 2 * KV_WIDTH
COL_Z = 0
COL_XS = 1
COL_GATES = 2
COL_Q = (3 * D_INNER) // ATTN_WIDTH
COL_BC = (3 * D_INNER + ATTN_WIDTH) // BC_WIDTH
COL_K = (3 * D_INNER + ATTN_WIDTH + BC_WIDTH) // KV_WIDTH
COL_V = COL_K + 1
DT_PAD = 128

LANES = 128
SUBLANES = 8
HALO_ROWS = 16
CONV_ROWS = 64
SSD_CHUNKS_PER_STEP = 4
ATTN_BLOCKS_PER_STEP = 4
FFN_CHUNK = 256
IN_PROJ_N_CHUNK = 256
ROW_TILE = 512
NEG_BIG = -1e30
VMEM_LIMIT = 56 * 1024 * 1024


def _rms(x, g):
    return x * lax.rsqrt(jnp.mean(x * x, axis=-1, keepdims=True) + EPS) * g


def _silu(x):
    return x * lax.logistic(x)


def _sigmoid(x):
    return lax.logistic(x)


def _softplus(x):
    return jnp.maximum(x, 0.0) + jnp.log1p(jnp.exp(-jnp.abs(x)))


def _dot(a, b):
    return jnp.dot(a, b, preferred_element_type=F32)


def _dot_nt(a, b):
    return lax.dot_general(a, b, (((1,), (1,)), ((), ())), preferred_element_type=F32)


def _resident(shape):
    nd = len(shape)
    return pl.BlockSpec(shape, lambda *_: (0,) * nd, pipeline_mode=pl.Buffered(1))


def _params(n_axes):
    return pltpu.CompilerParams(dimension_semantics=("arbitrary",) * n_axes,
                                vmem_limit_bytes=VMEM_LIMIT)


def _in_proj_kernel(tiles_per_seq, x_ref, xp_ref, xn_ref, g_ref, w_ref, wdt_ref, cw_ref, cb_ref,
                    o_ref, dt_ref, h_ext, ext):
    i = pl.program_id(0)
    tm = x_ref.shape[0]
    g = g_ref[...]
    h_ext[0:HALO_ROWS, :] = jnp.where(i % tiles_per_seq != 0, _rms(xp_ref[...], g), 0.0).astype(BF16)
    h_ext[HALO_ROWS:HALO_ROWS + tm, :] = _rms(x_ref[...], g).astype(BF16)
    h_ext[HALO_ROWS + tm:, :] = jnp.where((i + 1) % tiles_per_seq != 0, _rms(xn_ref[...], g), 0.0).astype(BF16)
    blk_rows = CONV_ROWS + 2 * SUBLANES

    def finish(n0, main):
        conv_col = _conv_column(n0)
        if conv_col is None:
            if n0 < (COL_Z + 1) * D_INNER:
                main = _silu(main)
            o_ref[:, n0:n0 + IN_PROJ_N_CHUNK] = main.astype(BF16)
            return
        for r0 in range(0, tm, CONV_ROWS):
            for l0 in range(0, IN_PROJ_N_CHUNK, LANES):
                cc = slice(conv_col + l0, conv_col + l0 + LANES)
                blk = ext[HALO_ROWS - SUBLANES + r0:HALO_ROWS - SUBLANES + r0 + blk_rows, l0:l0 + LANES]
                acc = cb_ref[:, cc] + cw_ref[CONV_PAD:CONV_PAD + 1, cc] * blk[SUBLANES:SUBLANES + CONV_ROWS]
                for t in range(CONV_K):
                    if t != CONV_PAD:
                        shifted = pltpu.roll(blk, (CONV_PAD - t) % blk_rows, axis=0)
                        acc = acc + cw_ref[t:t + 1, cc] * shifted[SUBLANES:SUBLANES + CONV_ROWS]
                o_ref[r0:r0 + CONV_ROWS, n0 + l0:n0 + l0 + LANES] = _silu(acc).astype(BF16)

    pending = None
    for n0 in _chunk_order():
        cols = slice(n0, n0 + IN_PROJ_N_CHUNK)
        if _conv_column(n0) is None:
            main = _dot(h_ext[HALO_ROWS:HALO_ROWS + tm, :], w_ref[:, cols])
        else:
            ext[...] = _dot(h_ext[...], w_ref[:, cols])
            main = None
        if pending is not None:
            finish(*pending)
        pending = (n0, main)
    dt_ref[...] = _dot(h_ext[HALO_ROWS:HALO_ROWS + tm, :], wdt_ref[...])
    finish(*pending)


def _chunk_order():
    chunks = list(range(0, PROJ_WIDTH, IN_PROJ_N_CHUNK))
    heavy = [n0 for n0 in chunks if _conv_column(n0) is not None]
    light = [n0 for n0 in chunks if _conv_column(n0) is None]
    order = []
    while heavy or light:
        if heavy:
            order.append(heavy.pop(0))
        take = -(-len(light) // (len(heavy) + 1))
        order.extend(light[:take])
        light = light[take:]
    return order


def _conv_column(n0):
    if COL_XS * D_INNER <= n0 < (COL_XS + 1) * D_INNER:
        return n0 - COL_XS * D_INNER
    if COL_BC * BC_WIDTH <= n0 < (COL_BC + 1) * BC_WIDTH:
        return D_INNER + n0 - COL_BC * BC_WIDTH
    return None


def _in_proj(x2d, w, tm, seq):
    m = x2d.shape[0]
    halo_per_tile = tm // HALO_ROWS
    n_halo = m // HALO_ROWS
    return pl.pallas_call(
        functools.partial(_in_proj_kernel, seq // tm),
        grid=(m // tm,),
        in_specs=[pl.BlockSpec((tm, D_MODEL), lambda i: (i, 0)),
                  pl.BlockSpec((HALO_ROWS, D_MODEL), lambda i: (jnp.maximum(i * halo_per_tile - 1, 0), 0)),
                  pl.BlockSpec((HALO_ROWS, D_MODEL),
                               lambda i: (jnp.minimum((i + 1) * halo_per_tile, n_halo - 1), 0)),
                  _resident((1, D_MODEL)),
                  _resident((D_MODEL, PROJ_WIDTH)),
                  _resident((D_MODEL, DT_PAD)),
                  _resident((CONV_K, D_INNER + BC_WIDTH)), _resident((1, D_INNER + BC_WIDTH))],
        out_specs=[pl.BlockSpec((tm, PROJ_WIDTH), lambda i: (i, 0)),
                   pl.BlockSpec((tm, DT_PAD), lambda i: (i, 0))],
        out_shape=[jax.ShapeDtypeStruct((m, PROJ_WIDTH), BF16),
                   jax.ShapeDtypeStruct((m, DT_PAD), F32)],
        scratch_shapes=[pltpu.VMEM((tm + 2 * HALO_ROWS, D_MODEL), BF16),
                        pltpu.VMEM((tm + 2 * HALO_ROWS, IN_PROJ_N_CHUNK), F32)],
        compiler_params=_params(1),
        name="in_proj",
    )(x2d, x2d, x2d, w["norm_mix"], w["w_main"], w["w_dt"], w["conv_w"], w["conv_b"])


def _attn_kernel(n_sub, *refs):
    for _ in _attn_steps(n_sub, *refs):
        pass


def _attn_steps(n_sub, q_ref, kp_ref, kc_ref, kn_ref, vp_ref, vc_ref, vn_ref, sink_ref, *refs):
    bias_refs, o_ref = refs[:n_sub], refs[n_sub]
    k_all = jnp.concatenate([kp_ref[0], kc_ref[0], kn_ref[0]], axis=0)
    v_all = jnp.concatenate([vp_ref[0], vc_ref[0], vn_ref[0]], axis=0)
    vt_all = v_all.astype(F32).T.astype(BF16)
    cols = Q_PER_KV * BLOCK
    ones_rows = (lax.broadcasted_iota(jnp.int32, (HALO_ROWS, SPAN), 0) == 0).astype(BF16)
    for sub in range(n_sub):
        q = q_ref[0, sub * BLOCK:(sub + 1) * BLOCK, :] * jnp.asarray(HEAD_DIM ** -0.5, BF16)
        k = k_all[sub * BLOCK:sub * BLOCK + SPAN]
        v_t = vt_all[:, sub * BLOCK:sub * BLOCK + SPAN]
        bias_ref = bias_refs[sub]

        def logits(g):
            q4 = jnp.concatenate([q[:, (g * Q_PER_KV + r) * HEAD_DIM:(g * Q_PER_KV + r + 1) * HEAD_DIM]
                                  for r in range(Q_PER_KV)], axis=0)
            return (_dot_nt(k[:, g * HEAD_DIM:(g + 1) * HEAD_DIM], q4)
                    + bias_ref[0, g * SPAN:(g + 1) * SPAN, :])

        outs = []
        s_next = logits(0)
        for g in range(N_KV_HEADS):
            s = s_next
            if g + 1 < N_KV_HEADS:
                s_next = logits(g + 1)
            sink = sink_ref[:, g * cols:(g + 1) * cols]
            m = jnp.maximum(jnp.max(s, axis=0, keepdims=True), sink)
            e = jnp.exp(s - m).astype(BF16)
            v_aug = jnp.concatenate([v_t[g * HEAD_DIM:(g + 1) * HEAD_DIM], ones_rows], axis=0)
            o_aug = _dot(v_aug, e)
            den = o_aug[HEAD_DIM:HEAD_DIM + 1] + jnp.exp(sink - m)
            o_t = o_aug[:HEAD_DIM] * (1.0 / den)
            outs.extend(o_t[:, r * BLOCK:(r + 1) * BLOCK] for r in range(Q_PER_KV))
            if g + 1 < N_KV_HEADS:
                yield
        o_ref[0, sub * BLOCK:(sub + 1) * BLOCK, :] = jnp.concatenate(outs, axis=0).T.astype(BF16)
        yield


def _window_attention(proj3d, bias, sink):
    b, s, _ = proj3d.shape
    nblk = s // BLOCK
    n_sub = ATTN_BLOCKS_PER_STEP if nblk % ATTN_BLOCKS_PER_STEP == 0 else 1
    nstep = nblk // n_sub
    in_specs, out_spec = _attention_specs(nblk, n_sub, lambda i: i)
    return pl.pallas_call(
        functools.partial(_attn_kernel, n_sub),
        grid=(b, nstep),
        in_specs=in_specs,
        out_specs=out_spec,
        out_shape=jax.ShapeDtypeStruct((b, s, ATTN_WIDTH), BF16),
        compiler_params=_params(2),
        name="window_attention",
    )(*_attention_operands(proj3d, bias, sink, n_sub))


def _attention_specs(nblk, n_sub, pos):
    def edge_spec(col, index):
        return pl.BlockSpec((1, BLOCK, KV_WIDTH), lambda bi, i: (bi, index(pos(i)), col))

    def kv_specs(col):
        return [edge_spec(col, lambda p: jnp.maximum(p * n_sub - 1, 0)),
                pl.BlockSpec((1, n_sub * BLOCK, KV_WIDTH), lambda bi, i: (bi, pos(i), col)),
                edge_spec(col, lambda p: jnp.minimum((p + 1) * n_sub, nblk - 1))]

    def bias_spec(sub):
        def variant(bi, i):
            first = (pos(i) * n_sub + sub == 0).astype(jnp.int32)
            last = (pos(i) * n_sub + sub == nblk - 1).astype(jnp.int32)
            return (first + 2 * last, 0, 0)
        return pl.BlockSpec((1, N_KV_HEADS * SPAN, Q_PER_KV * BLOCK), variant)

    in_specs = [pl.BlockSpec((1, n_sub * BLOCK, ATTN_WIDTH), lambda bi, i: (bi, pos(i), COL_Q)),
                *kv_specs(COL_K), *kv_specs(COL_V), _resident((1, N_HEADS * BLOCK)),
                *[bias_spec(sub) for sub in range(n_sub)]]
    return in_specs, pl.BlockSpec((1, n_sub * BLOCK, ATTN_WIDTH), lambda bi, i: (bi, pos(i), 0))


def _attention_operands(proj3d, bias, sink, n_sub):
    return [proj3d] * 7 + [sink] + [bias] * n_sub


def _split_hi_lo(v):
    hi = v.astype(BF16)
    return hi, (v - hi.astype(F32)).astype(BF16)


def _scan_tables(forward, dt_raw, dtb_ref, alog_ref, e2_ref):
    total_row = CHUNK - 1 if forward else 0
    dtv = _softplus(dt_raw + dtb_ref[...])
    adt = dtv * (-jnp.exp(alog_ref[...])) * LOG2_E
    ii = lax.broadcasted_iota(jnp.int32, (CHUNK, CHUNK), 0)
    jj = lax.broadcasted_iota(jnp.int32, (CHUNK, CHUNK), 1)
    scanned = (jj <= ii) if forward else (jj >= ii)
    tri = scanned.astype(BF16)
    a1 = adt.astype(BF16)
    r1 = adt - a1.astype(F32)
    a2 = r1.astype(BF16)
    a3 = (r1 - a2.astype(F32)).astype(BF16)
    pc = _dot(jnp.concatenate([tri, tri, tri], axis=1), jnp.concatenate([a1, a2, a3], axis=0))
    total = pc[total_row:total_row + 1, :]
    both = jnp.concatenate([jnp.exp2(pc), jnp.exp2(total - pc) * dtv], axis=0)
    hi, lo = _split_hi_lo(both)
    both_e = _dot(jnp.concatenate([hi, lo], axis=1), e2_ref[...])
    off_e = both_e[:CHUNK]
    w_e = both_e[CHUNK:]
    decay = off_e[total_row:total_row + 1, :]
    row_t = (pc - jnp.log(dtv) * LOG2_E).T
    return scanned, pc, row_t, off_e, w_e, decay


def _scan_group(g, lane0, scanned, pc, row_t, off_e, decay, b_t, c_bf, x_bf, xw_bf, state):
    gcols = slice(g * GROUP_WIDTH, (g + 1) * GROUP_WIDTH)
    gmat = _dot(c_bf, b_t)
    sg = state[g]
    y_off = _dot(c_bf, sg.astype(BF16)) * off_e[:, gcols]

    def decay_block(r):
        lane = lane0 + g * HEADS_PER_GROUP + r
        delta = pc[:, lane:lane + 1] - row_t[lane:lane + 1, :]
        return (gmat * jnp.exp2(jnp.where(scanned, delta, NEG_BIG))).astype(BF16)

    first_head = lax.broadcasted_iota(jnp.int32, (1, LANES), 1) < SSD_HEAD_DIM
    ys = []
    for q in range(HEADS_PER_GROUP // 2):
        x_pair = x_bf[:, q * LANES:(q + 1) * LANES]
        zero = jnp.zeros_like(x_pair)
        rhs = jnp.concatenate([jnp.where(first_head, x_pair, zero), jnp.where(first_head, zero, x_pair)],
                              axis=0)
        ys.append(_dot(jnp.concatenate([decay_block(2 * q), decay_block(2 * q + 1)], axis=1), rhs))
    state[g] = sg * decay[:, gcols] + _dot(b_t, xw_bf)
    return jnp.concatenate(ys, axis=-1) + y_off


def _reset_state_at_row_start(state):
    @pl.when(pl.program_id(1) == 0)
    def _():
        state[...] = jnp.zeros_like(state)


def _ssd_bwd_kernel(*refs):
    _reset_state_at_row_start(refs[-1])
    for _ in _ssd_bwd_steps(*refs):
        pass


def _ssd_bwd_steps(x_ref, bc_ref, dt_ref, dtb_ref, alog_ref, e2_ref, y_ref, bt_ref, state):
    order = list(reversed(range(x_ref.shape[1] // CHUNK)))
    tabs = {ci: _scan_tables(False, dt_ref[0, ci * CHUNK:(ci + 1) * CHUNK, :], dtb_ref, alog_ref, e2_ref)
            for ci in order}
    bc_half = BC_WIDTH // 2
    for ci in order:
        rows = slice(ci * CHUNK, (ci + 1) * CHUNK)
        scanned, pc, row_t, off_e, w_e, decay = tabs[ci]
        for g in range(SSD_GROUPS):
            gcols = slice(g * GROUP_WIDTH, (g + 1) * GROUP_WIDTH)
            x_bf = x_ref[0, rows, gcols]
            b_t = bc_ref[0, rows, g * D_STATE:(g + 1) * D_STATE].astype(F32).T.astype(BF16)
            bt_ref[0, ci * bc_half + g * D_STATE:ci * bc_half + (g + 1) * D_STATE, :] = b_t
            c_bf = bc_ref[0, rows, (SSD_GROUPS + g) * D_STATE:(SSD_GROUPS + g + 1) * D_STATE]
            y = _scan_group(g, SSD_HEADS, scanned, pc, row_t, off_e, decay, b_t, c_bf, x_bf,
                            x_bf * w_e[:, gcols].astype(BF16), state)
            y_ref[0, rows, gcols] = y.astype(BF16)
            yield


def _ssd_fwd_kernel(x_ref, c_ref, bt_ref, dt_ref, dtb_ref, alog_ref, e2_ref, yb_ref, z_ref, dskip_ref,
                    gn_ref, out_ref, state, ybuf):
    @pl.when(pl.program_id(1) == 0)
    def _():
        state[...] = jnp.zeros_like(state)

    order = list(range(x_ref.shape[1] // CHUNK))
    tabs = {ci: _scan_tables(True, dt_ref[0, ci * CHUNK:(ci + 1) * CHUNK, :], dtb_ref, alog_ref, e2_ref)
            for ci in order}
    bc_half = BC_WIDTH // 2
    for ci in order:
        rows = slice(ci * CHUNK, (ci + 1) * CHUNK)
        scanned, pc, row_t, off_e, w_e, decay = tabs[ci]
        for g in range(SSD_GROUPS):
            gcols = slice(g * GROUP_WIDTH, (g + 1) * GROUP_WIDTH)
            x_bf = x_ref[0, rows, gcols]
            xg = x_bf.astype(F32)
            b_t = bt_ref[0, ci * bc_half + g * D_STATE:ci * bc_half + (g + 1) * D_STATE, :]
            c_bf = c_ref[0, rows, g * D_STATE:(g + 1) * D_STATE]
            y = _scan_group(g, 0, scanned, pc, row_t, off_e, decay, b_t, c_bf, x_bf,
                            x_bf * w_e[:, gcols].astype(BF16), state)
            ybuf[rows, gcols] = y + yb_ref[0, rows, gcols].astype(F32) + xg * dskip_ref[:, gcols]
        gated = ybuf[rows, :] * z_ref[0, rows, :].astype(F32)
        out_ref[0, rows, :] = _rms(gated, gn_ref[...]).astype(BF16)


def _attn_ssd_bwd_kernel(n_sub, *refs):
    n_attn_in = 8 + n_sub
    n_ssd_in = 6
    o_ref, y_ref, bt_ref, state = refs[n_attn_in + n_ssd_in:]
    _reset_state_at_row_start(state)
    attn = _attn_steps(n_sub, *refs[:n_attn_in], o_ref)
    ssd = _ssd_bwd_steps(*refs[n_attn_in:n_attn_in + n_ssd_in], y_ref, bt_ref, state)
    for _ in itertools.zip_longest(attn, ssd):
        pass


def _mixer_calls(proj3d, dt3d, tables, w):
    b, s, _ = proj3d.shape
    nchunk = s // CHUNK
    per_step = SSD_CHUNKS_PER_STEP if nchunk % SSD_CHUNKS_PER_STEP == 0 else 1
    nstep = nchunk // per_step
    rows = per_step * CHUNK
    bc_half = BC_WIDTH // 2
    fused = per_step == ATTN_BLOCKS_PER_STEP and BLOCK == CHUNK

    def rev_spec(width, col=0):
        return pl.BlockSpec((1, rows, width), lambda bi, c: (bi, nstep - 1 - c, col))

    def chunk_spec(width, col=0):
        return pl.BlockSpec((1, rows, width), lambda bi, c: (bi, c, col))

    state = pltpu.VMEM((SSD_GROUPS, D_STATE, GROUP_WIDTH), F32)
    bwd_in_specs = [rev_spec(D_INNER, COL_XS), rev_spec(BC_WIDTH, COL_BC), rev_spec(DT_PAD),
                    _resident((1, DT_PAD)), _resident((1, DT_PAD)), _resident((2 * DT_PAD, D_INNER))]
    bwd_out_specs = [rev_spec(D_INNER),
                     pl.BlockSpec((1, per_step * bc_half, CHUNK), lambda bi, c: (bi, nstep - 1 - c, 0))]
    bwd_out_shape = [jax.ShapeDtypeStruct((b, s, D_INNER), BF16),
                     jax.ShapeDtypeStruct((b, nchunk * bc_half, CHUNK), BF16)]
    bwd_operands = [proj3d, proj3d, dt3d, w["dt_bias"], w["a_log"], tables["expand_bwd"]]
    if fused:
        attn_in_specs, attn_out_spec = _attention_specs(nchunk, per_step, lambda c: nstep - 1 - c)
        attn_o, y_bwd, bt = pl.pallas_call(
            functools.partial(_attn_ssd_bwd_kernel, per_step),
            grid=(b, nstep),
            in_specs=attn_in_specs + bwd_in_specs,
            out_specs=[attn_out_spec] + bwd_out_specs,
            out_shape=[jax.ShapeDtypeStruct((b, s, ATTN_WIDTH), BF16)] + bwd_out_shape,
            scratch_shapes=[state],
            compiler_params=_params(2),
            name="attention_ssd_backward",
        )(*_attention_operands(proj3d, tables["attn_bias"], w["attn_sink"], per_step), *bwd_operands)
    else:
        attn_o = _window_attention(proj3d, tables["attn_bias"], w["attn_sink"])
        y_bwd, bt = pl.pallas_call(
            _ssd_bwd_kernel,
            grid=(b, nstep),
            in_specs=bwd_in_specs,
            out_specs=bwd_out_specs,
            out_shape=bwd_out_shape,
            scratch_shapes=[state],
            compiler_params=_params(2),
            name="ssd_backward",
        )(*bwd_operands)

    ssd_o = pl.pallas_call(
        _ssd_fwd_kernel,
        grid=(b, nstep),
        in_specs=[chunk_spec(D_INNER, COL_XS), chunk_spec(bc_half, COL_BC * 2 + 1),
                  pl.BlockSpec((1, per_step * bc_half, CHUNK), lambda bi, c: (bi, c, 0)),
                  chunk_spec(DT_PAD),
                  _resident((1, DT_PAD)), _resident((1, DT_PAD)), _resident((2 * DT_PAD, D_INNER)),
                  chunk_spec(D_INNER), chunk_spec(D_INNER, COL_Z),
                  _resident((1, D_INNER)), _resident((1, D_INNER))],
        out_specs=chunk_spec(D_INNER),
        out_shape=jax.ShapeDtypeStruct((b, s, D_INNER), BF16),
        scratch_shapes=[state, pltpu.VMEM((rows, D_INNER), F32)],
        compiler_params=_params(2),
        name="ssd_forward",
    )(proj3d, proj3d, bt, dt3d, w["dt_bias"], w["a_log"], tables["expand_fwd"], y_bwd, proj3d,
      w["d_skip"], w["norm_ssd"])
    return attn_o, ssd_o


def _mem_kv_kernel(mem_ref, g_ref, w_ref, o_ref):
    h = _rms(mem_ref[0], g_ref[...]).astype(BF16)
    o_ref[0] = _dot(h, w_ref[...]).astype(BF16)


def _mem_kv(mem, g, w_kv):
    b = mem.shape[0]
    return pl.pallas_call(
        _mem_kv_kernel,
        grid=(b,),
        in_specs=[pl.BlockSpec((1, MEM_LEN, D_MODEL), lambda i: (i, 0, 0)),
                  _resident((1, D_MODEL)), _resident((D_MODEL, 2 * D_MODEL))],
        out_specs=pl.BlockSpec((1, MEM_LEN, 2 * D_MODEL), lambda i: (i, 0, 0)),
        out_shape=jax.ShapeDtypeStruct((b, MEM_LEN, 2 * D_MODEL), BF16),
        compiler_params=_params(1),
        name="memory_kv",
    )(mem, g, w_kv)


def _merge_cross_kernel(ssd_ref, att_ref, gate_ref, x_ref, kv_ref, wpa_ref, wps_ref, wout_ref,
                        gc_ref, wq_ref, wo_ref, o_ref):
    gates = _sigmoid(gate_ref[...].astype(F32))
    mixed = (gates[:, :D_MODEL] * _dot(att_ref[...], wpa_ref[...])
             + gates[:, D_MODEL:] * _dot(ssd_ref[...], wps_ref[...]))
    x1 = x_ref[...] + _dot(mixed.astype(BF16), wout_ref[...])
    h = _rms(x1, gc_ref[...]).astype(BF16)
    q = (_dot(h, wq_ref[...]) * CROSS_HEAD_DIM ** -0.5).astype(BF16)
    outs = []
    for hh in range(CROSS_HEADS):
        hs = slice(hh * CROSS_HEAD_DIM, (hh + 1) * CROSS_HEAD_DIM)
        vs = slice(D_MODEL + hh * CROSS_HEAD_DIM, D_MODEL + (hh + 1) * CROSS_HEAD_DIM)
        s = _dot_nt(q[:, hs], kv_ref[0, :, hs])
        e = jnp.exp(s - jnp.max(s, axis=-1, keepdims=True))
        den = jnp.sum(e, axis=-1, keepdims=True)
        outs.append(_dot(e.astype(BF16), kv_ref[0, :, vs]) * (1.0 / den))
    o = jnp.concatenate(outs, axis=-1).astype(BF16)
    o_ref[...] = x1 + _dot(o, wo_ref[...])


def _merge_cross(ssd_o, attn_o, proj2d, x2d, kv, w, tm, seq):
    m = x2d.shape[0]
    tiles_per_seq = seq // tm
    return pl.pallas_call(
        _merge_cross_kernel,
        grid=(m // tm,),
        in_specs=[pl.BlockSpec((tm, D_INNER), lambda i: (i, 0)),
                  pl.BlockSpec((tm, ATTN_WIDTH), lambda i: (i, 0)),
                  pl.BlockSpec((tm, 2 * D_MODEL), lambda i: (i, COL_GATES)),
                  pl.BlockSpec((tm, D_MODEL), lambda i: (i, 0)),
                  pl.BlockSpec((1, MEM_LEN, 2 * D_MODEL), lambda i: (i // tiles_per_seq, 0, 0)),
                  _resident((ATTN_WIDTH, D_MODEL)), _resident((D_INNER, D_MODEL)),
                  _resident((D_MODEL, D_MODEL)), _resident((1, D_MODEL)),
                  _resident((D_MODEL, D_MODEL)), _resident((D_MODEL, D_MODEL))],
        out_specs=pl.BlockSpec((tm, D_MODEL), lambda i: (i, 0)),
        out_shape=jax.ShapeDtypeStruct((m, D_MODEL), F32),
        compiler_params=_params(1),
        name="merge_cross",
    )(ssd_o, attn_o, proj2d, x2d, kv, w["w_proj_attn"], w["w_proj_ssd"], w["w_out"],
      w["norm_cross"], w["w_q_cross"], w["w_o_cross"])


def _ffn_chunks():
    chunks, f0 = [], 0
    while f0 < D_FF:
        fw = min(FFN_CHUNK, D_FF - f0)
        chunks.append((f0, fw))
        f0 += fw
    return chunks


def _ffn_kernel(final_norm, x_ref, g_ref, wgu_ref, wd_ref, gf_ref, o_ref, acc_ref):
    x = x_ref[...]
    h = _rms(x, g_ref[...]).astype(BF16)
    for n, (f0, fw) in enumerate(_ffn_chunks()):
        gate = _dot(h, wgu_ref[:, f0:f0 + fw])
        up = _dot(h, wgu_ref[:, D_FF + f0:D_FF + f0 + fw])
        part = _dot((_silu(gate) * up).astype(BF16), wd_ref[f0:f0 + fw, :])
        if n == 0:
            acc_ref[...] = x + part
        else:
            acc_ref[...] += part
    y = acc_ref[...]
    o_ref[...] = _rms(y, gf_ref[...]) if final_norm else y


def _ffn(x2d, g, w_gate_up, w_down, g_final, final_norm, tm):
    m = x2d.shape[0]
    return pl.pallas_call(
        functools.partial(_ffn_kernel, final_norm),
        grid=(m // tm,),
        in_specs=[pl.BlockSpec((tm, D_MODEL), lambda i: (i, 0)),
                  _resident((1, D_MODEL)),
                  _resident((D_MODEL, 2 * D_FF)), _resident((D_FF, D_MODEL)),
                  _resident((1, D_MODEL))],
        out_specs=pl.BlockSpec((tm, D_MODEL), lambda i: (i, 0)),
        out_shape=jax.ShapeDtypeStruct((m, D_MODEL), F32),
        scratch_shapes=[pltpu.VMEM((tm, D_MODEL), F32)],
        compiler_params=_params(1),
        name="ffn_final" if final_norm else "ffn",
    )(x2d, g, w_gate_up, w_down, g_final)


def _t5_bucket(rel):
    nb = NUM_BUCKETS // 2
    max_exact = nb // 2
    ret = jnp.where(rel > 0, nb, 0)
    n = jnp.abs(rel)
    nf = jnp.maximum(n, 1).astype(F32)
    large = max_exact + (jnp.log(nf / max_exact) / math.log(MAX_DISTANCE / max_exact)
                         * (nb - max_exact)).astype(jnp.int32)
    large = jnp.minimum(large, nb - 1)
    return ret + jnp.where(n < max_exact, n, large)


def _attention_bias(rel_bias):
    rel = jnp.arange(SPAN)[None, :] - WINDOW - jnp.arange(BLOCK)[:, None]
    one_hot = (_t5_bucket(rel)[None] == jnp.arange(NUM_BUCKETS)[:, None, None]).astype(F32)
    bias = jnp.einsum("bh,bqk->hqk", rel_bias.astype(F32), one_hot, precision=lax.Precision.HIGHEST)
    bias = jnp.where((jnp.abs(rel) <= WINDOW)[None], bias, NEG_BIG)
    col = jnp.arange(SPAN)
    variants = []
    for first, last in ((False, False), (True, False), (False, True), (True, True)):
        pad = (first & (col < WINDOW)) | (last & (col >= WINDOW + BLOCK))
        masked = jnp.where(pad[None, None, :], NEG_BIG, bias).reshape(N_KV_HEADS, Q_PER_KV, BLOCK, SPAN)
        variants.append(jnp.transpose(masked, (0, 3, 1, 2)).reshape(N_KV_HEADS * SPAN, Q_PER_KV * BLOCK))
    return jnp.stack(variants)


def _head_expand_matrix(lane0):
    e = np.zeros((2 * DT_PAD, D_INNER), np.float32)
    for h in range(SSD_HEADS):
        e[lane0 + h, h * SSD_HEAD_DIM:(h + 1) * SSD_HEAD_DIM] = 1.0
        e[DT_PAD + lane0 + h, h * SSD_HEAD_DIM:(h + 1) * SSD_HEAD_DIM] = 1.0
    return jnp.asarray(e, BF16)


def _tables(rel_bias):
    return {"attn_bias": _attention_bias(rel_bias), "expand_fwd": _head_expand_matrix(0),
            "expand_bwd": _head_expand_matrix(SSD_HEADS)}


def _prepare_layer(l, p):
    w_in = p["w_in"][l]
    o_q, o_k, o_v = 0, ATTN_WIDTH, ATTN_WIDTH + KV_WIDTH
    o_z = ATTN_WIDTH + 2 * KV_WIDTH
    o_xs = o_z + D_INNER
    o_bc = o_xs + D_INNER
    o_dt = o_bc + BC_WIDTH
    o_g = o_dt + 2 * SSD_HEADS
    w_main = jnp.concatenate([w_in[:, o_z:o_xs], w_in[:, o_xs:o_bc], w_in[:, o_g:o_g + 2 * D_MODEL],
                              w_in[:, o_q:o_k], w_in[:, o_bc:o_dt], w_in[:, o_k:o_v],
                              w_in[:, o_v:o_z]], axis=1).astype(BF16)
    w_dt = jnp.pad(w_in[:, o_dt:o_g], ((0, 0), (0, DT_PAD - 2 * SSD_HEADS))).astype(BF16)
    pad_dt = lambda a: jnp.pad(a.reshape(1, 2 * SSD_HEADS), ((0, 0), (0, DT_PAD - 2 * SSD_HEADS)))
    row = lambda a: a.reshape(1, -1).astype(F32)
    return {
        "norm_mix": row(p["norm_mix"][l]), "w_main": w_main, "w_dt": w_dt,
        "attn_sink": jnp.repeat(p["attn_sink"][l].astype(F32), BLOCK).reshape(1, N_HEADS * BLOCK),
        "conv_w": p["conv_w"][l].astype(F32), "conv_b": row(p["conv_b"][l]),
        "dt_bias": pad_dt(p["dt_bias"][l]), "a_log": pad_dt(p["a_log"][l]),
        "d_skip": row(jnp.repeat(p["d_skip"][l], SSD_HEAD_DIM)),
        "norm_ssd": row(p["norm_ssd"][l]),
        "w_proj_attn": p["w_proj_attn"][l].astype(BF16), "w_proj_ssd": p["w_proj_ssd"][l].astype(BF16),
        "w_out": p["w_out"][l].astype(BF16), "norm_cross": row(p["norm_cross"][l]),
        "norm_mem": row(p["norm_mem"][l]), "w_q_cross": p["w_q_cross"][l].astype(BF16),
        "w_kv_cross": p["w_kv_cross"][l].astype(BF16), "w_o_cross": p["w_o_cross"][l].astype(BF16),
        "norm_ffn": row(p["norm_ffn"][l]), "w_gate_up": p["w_gate_up"][l].astype(BF16),
        "w_down": p["w_down"][l].astype(BF16),
    }


def _encoder(x, mem, layers, tables, norm_final):
    b, s, _ = x.shape
    m = b * s
    tm = min(ROW_TILE, s)
    x2d = x.reshape(m, D_MODEL)
    for l, w in enumerate(layers):
        proj, dt = _in_proj(x2d, w, tm, s)
        proj3d = proj.reshape(b, s, PROJ_WIDTH)
        attn_o, ssd_o = _mixer_calls(proj3d, dt.reshape(b, s, DT_PAD), tables, w)
        kv = _mem_kv(mem, w["norm_mem"], w["w_kv_cross"])
        x2d = _merge_cross(ssd_o.reshape(m, D_INNER), attn_o.reshape(m, ATTN_WIDTH), proj, x2d, kv, w,
                           tm, s)
        x2d = _ffn(x2d, w["norm_ffn"], w["w_gate_up"], w["w_down"], norm_final,
                   l == len(layers) - 1, tm)
    return x2d.reshape(b, s, D_MODEL)


def kernel(x_prompt, x_sample, mem_prompt, mem_sample, rel_bias, norm_mix, w_in, attn_sink, conv_w,
           conv_b, dt_bias, a_log, d_skip, norm_ssd, w_proj_attn, w_proj_ssd, w_out, norm_cross,
           norm_mem, w_q_cross, w_kv_cross, w_o_cross, norm_ffn, w_gate_up, w_down, norm_final):
    p = dict(norm_mix=norm_mix, w_in=w_in, attn_sink=attn_sink, conv_w=conv_w, conv_b=conv_b,
             dt_bias=dt_bias, a_log=a_log, d_skip=d_skip, norm_ssd=norm_ssd, w_proj_attn=w_proj_attn,
             w_proj_ssd=w_proj_ssd, w_out=w_out, norm_cross=norm_cross, norm_mem=norm_mem,
             w_q_cross=w_q_cross, w_kv_cross=w_kv_cross, w_o_cross=w_o_cross, norm_ffn=norm_ffn,
             w_gate_up=w_gate_up, w_down=w_down)
    layers = [_prepare_layer(l, p) for l in range(norm_mix.shape[0])]
    tables = _tables(rel_bias)
    g_final = norm_final.reshape(1, D_MODEL).astype(F32)
    y_prompt = _encoder(x_prompt, mem_prompt, layers, tables, g_final)
    y_sample = _encoder(x_sample, mem_sample, layers, tables, g_final)
    return (y_prompt, y_sample)
```

```python
import functools
import itertools
import math

import numpy as np
import jax
import jax.numpy as jnp
from jax import lax
from jax.experimental import pallas as pl
from jax.experimental.pallas import tpu as pltpu

F32 = jnp.float32
BF16 = jnp.bfloat16

D_MODEL = 1024
N_HEADS = 16
N_KV_HEADS = 4
HEAD_DIM = 64
Q_PER_KV = N_HEADS // N_KV_HEADS
ATTN_WIDTH = N_HEADS * HEAD_DIM
KV_WIDTH = N_KV_HEADS * HEAD_DIM
WINDOW = 128
BLOCK = 128
SPAN = BLOCK + 2 * WINDOW
NUM_BUCKETS = 32
MAX_DISTANCE = 128
D_INNER = 2 * D_MODEL
SSD_HEAD_DIM = 64
SSD_HEADS = D_INNER // SSD_HEAD_DIM
SSD_GROUPS = 4
HEADS_PER_GROUP = SSD_HEADS // SSD_GROUPS
GROUP_WIDTH = HEADS_PER_GROUP * SSD_HEAD_DIM
D_STATE = 128
BC_WIDTH = 2 * SSD_GROUPS * D_STATE
CONV_K = 5
CONV_PAD = CONV_K // 2
CHUNK = 128
MEM_LEN = 256
CROSS_HEADS = 4
CROSS_HEAD_DIM = D_MODEL // CROSS_HEADS
D_FF = ((8 * D_MODEL // 3 + 255) // 256) * 256
EPS = 1e-6
LOG2_E = math.log2(math.e)

PROJ_WIDTH = 2 * D_INNER + 2 * D_MODEL + ATTN_WIDTH + BC_WIDTH + 2 * KV_WIDTH
COL_Z = 0
COL_XS = 1
COL_GATES = 2
COL_Q = (3 * D_INNER) // ATTN_WIDTH
COL_BC = (3 * D_INNER + ATTN_WIDTH) // BC_WIDTH
COL_K = (3 * D_INNER + ATTN_WIDTH + BC_WIDTH) // KV_WIDTH
COL_V = COL_K + 1
DT_PAD = 128

LANES = 128
SUBLANES = 8
HALO_ROWS = 16
CONV_ROWS = 32
SSD_CHUNKS_PER_STEP = 4
ATTN_BLOCKS_PER_STEP = 4
FFN_CHUNK = 256
IN_PROJ_N_CHUNK = 256
ROW_TILE = 512
NEG_BIG = -1e30
VMEM_LIMIT = 56 * 1024 * 1024


def _rms(x, g):
    return x * lax.rsqrt(jnp.mean(x * x, axis=-1, keepdims=True) + EPS) * g


def _silu(x):
    return x * lax.logistic(x)


def _sigmoid(x):
    return lax.logistic(x)


def _softplus(x):
    return jnp.maximum(x, 0.0) + jnp.log1p(jnp.exp(-jnp.abs(x)))


def _dot(a, b):
    return jnp.dot(a, b, preferred_element_type=F32)


def _dot_nt(a, b):
    return lax.dot_general(a, b, (((1,), (1,)), ((), ())), preferred_element_type=F32)


def _resident(shape):
    nd = len(shape)
    return pl.BlockSpec(shape, lambda *_: (0,) * nd, pipeline_mode=pl.Buffered(1))


def _params(n_axes):
    return pltpu.CompilerParams(dimension_semantics=("arbitrary",) * n_axes,
                                vmem_limit_bytes=VMEM_LIMIT)


def _in_proj_kernel(tiles_per_seq, x_ref, xp_ref, xn_ref, g_ref, w_ref, wdt_ref, cw_ref, cb_ref,
                    o_ref, dt_ref, h_ext, ext):
    i = pl.program_id(0)
    tm = x_ref.shape[0]
    g = g_ref[...]
    h_ext[0:HALO_ROWS, :] = jnp.where(i % tiles_per_seq != 0, _rms(xp_ref[...], g), 0.0).astype(BF16)
    h_ext[HALO_ROWS:HALO_ROWS + tm, :] = _rms(x_ref[...], g).astype(BF16)
    h_ext[HALO_ROWS + tm:, :] = jnp.where((i + 1) % tiles_per_seq != 0, _rms(xn_ref[...], g), 0.0).astype(BF16)
    blk_rows = CONV_ROWS + 2 * SUBLANES

    def finish(n0, main):
        conv_col = _conv_column(n0)
        if conv_col is None:
            if n0 < (COL_Z + 1) * D_INNER:
                main = _silu(main)
            o_ref[:, n0:n0 + IN_PROJ_N_CHUNK] = main.astype(BF16)
            return
        for r0 in range(0, tm, CONV_ROWS):
            for l0 in range(0, IN_PROJ_N_CHUNK, LANES):
                cc = slice(conv_col + l0, conv_col + l0 + LANES)
                blk = ext[HALO_ROWS - SUBLANES + r0:HALO_ROWS - SUBLANES + r0 + blk_rows, l0:l0 + LANES]
                acc = cb_ref[:, cc] + cw_ref[CONV_PAD:CONV_PAD + 1, cc] * blk[SUBLANES:SUBLANES + CONV_ROWS]
                for t in range(CONV_K):
                    if t != CONV_PAD:
                        shifted = pltpu.roll(blk, (CONV_PAD - t) % blk_rows, axis=0)
                        acc = acc + cw_ref[t:t + 1, cc] * shifted[SUBLANES:SUBLANES + CONV_ROWS]
                o_ref[r0:r0 + CONV_ROWS, n0 + l0:n0 + l0 + LANES] = _silu(acc).astype(BF16)

    pending = None
    for n0 in _chunk_order():
        cols = slice(n0, n0 + IN_PROJ_N_CHUNK)
        if _conv_column(n0) is None:
            main = _dot(h_ext[HALO_ROWS:HALO_ROWS + tm, :], w_ref[:, cols])
        else:
            ext[...] = _dot(h_ext[...], w_ref[:, cols])
            main = None
        if pending is not None:
            finish(*pending)
        pending = (n0, main)
    dt_ref[...] = _dot(h_ext[HALO_ROWS:HALO_ROWS + tm, :], wdt_ref[...])
    finish(*pending)


def _chunk_order():
    chunks = list(range(0, PROJ_WIDTH, IN_PROJ_N_CHUNK))
    heavy = [n0 for n0 in chunks if _conv_column(n0) is not None]
    light = [n0 for n0 in chunks if _conv_column(n0) is None]
    order = []
    while heavy or light:
        if heavy:
            order.append(heavy.pop(0))
        take = -(-len(light) // (len(heavy) + 1))
        order.extend(light[:take])
        light = light[take:]
    return order


def _conv_column(n0):
    if COL_XS * D_INNER <= n0 < (COL_XS + 1) * D_INNER:
        return n0 - COL_XS * D_INNER
    if COL_BC * BC_WIDTH <= n0 < (COL_BC + 1) * BC_WIDTH:
        return D_INNER + n0 - COL_BC * BC_WIDTH
    return None


def _in_proj(x2d, w, tm, seq):
    m = x2d.shape[0]
    halo_per_tile = tm // HALO_ROWS
    n_halo = m // HALO_ROWS
    return pl.pallas_call(
        functools.partial(_in_proj_kernel, seq // tm),
        grid=(m // tm,),
        in_specs=[pl.BlockSpec((tm, D_MODEL), lambda i: (i, 0)),
                  pl.BlockSpec((HALO_ROWS, D_MODEL), lambda i: (jnp.maximum(i * halo_per_tile - 1, 0), 0)),
                  pl.BlockSpec((HALO_ROWS, D_MODEL),
                               lambda i: (jnp.minimum((i + 1) * halo_per_tile, n_halo - 1), 0)),
                  _resident((1, D_MODEL)),
                  _resident((D_MODEL, PROJ_WIDTH)),
                  _resident((D_MODEL, DT_PAD)),
                  _resident((CONV_K, D_INNER + BC_WIDTH)), _resident((1, D_INNER + BC_WIDTH))],
        out_specs=[pl.BlockSpec((tm, PROJ_WIDTH), lambda i: (i, 0)),
                   pl.BlockSpec((tm, DT_PAD), lambda i: (i, 0))],
        out_shape=[jax.ShapeDtypeStruct((m, PROJ_WIDTH), BF16),
                   jax.ShapeDtypeStruct((m, DT_PAD), F32)],
        scratch_shapes=[pltpu.VMEM((tm + 2 * HALO_ROWS, D_MODEL), BF16),
                        pltpu.VMEM((tm + 2 * HALO_ROWS, IN_PROJ_N_CHUNK), F32)],
        compiler_params=_params(1),
        name="in_proj",
    )(x2d, x2d, x2d, w["norm_mix"], w["w_main"], w["w_dt"], w["conv_w"], w["conv_b"])


def _attn_kernel(n_sub, *refs):
    for _ in _attn_steps(n_sub, *refs):
        pass


def _attn_steps(n_sub, q_ref, kp_ref, kc_ref, kn_ref, vp_ref, vc_ref, vn_ref, sink_ref, *refs):
    bias_refs, o_ref = refs[:n_sub], refs[n_sub]
    k_all = jnp.concatenate([kp_ref[0], kc_ref[0], kn_ref[0]], axis=0)
    v_all = jnp.concatenate([vp_ref[0], vc_ref[0], vn_ref[0]], axis=0)
    vt_all = v_all.astype(F32).T.astype(BF16)
    cols = Q_PER_KV * BLOCK
    ones_rows = (lax.broadcasted_iota(jnp.int32, (HALO_ROWS, SPAN), 0) == 0).astype(BF16)
    for sub in range(n_sub):
        q = q_ref[0, sub * BLOCK:(sub + 1) * BLOCK, :] * jnp.asarray(HEAD_DIM ** -0.5, BF16)
        k = k_all[sub * BLOCK:sub * BLOCK + SPAN]
        v_t = vt_all[:, sub * BLOCK:sub * BLOCK + SPAN]
        bias_ref = bias_refs[sub]

        def logits(g):
            q4 = jnp.concatenate([q[:, (g * Q_PER_KV + r) * HEAD_DIM:(g * Q_PER_KV + r + 1) * HEAD_DIM]
                                  for r in range(Q_PER_KV)], axis=0)
            return (_dot_nt(k[:, g * HEAD_DIM:(g + 1) * HEAD_DIM], q4)
                    + bias_ref[0, g * SPAN:(g + 1) * SPAN, :])

        outs = []
        s_next = logits(0)
        for g in range(N_KV_HEADS):
            s = s_next
            if g + 1 < N_KV_HEADS:
                s_next = logits(g + 1)
            sink = sink_ref[:, g * cols:(g + 1) * cols]
            m = jnp.maximum(jnp.max(s, axis=0, keepdims=True), sink)
            e = jnp.exp(s - m).astype(BF16)
            v_aug = jnp.concatenate([v_t[g * HEAD_DIM:(g + 1) * HEAD_DIM], ones_rows], axis=0)
            o_aug = _dot(v_aug, e)
            den = o_aug[HEAD_DIM:HEAD_DIM + 1] + jnp.exp(sink - m)
            o_t = o_aug[:HEAD_DIM] * (1.0 / den)
            outs.extend(o_t[:, r * BLOCK:(r + 1) * BLOCK] for r in range(Q_PER_KV))
            if g + 1 < N_KV_HEADS:
                yield
        o_ref[0, sub * BLOCK:(sub + 1) * BLOCK, :] = jnp.concatenate(outs, axis=0).T.astype(BF16)
        yield


def _window_attention(proj3d, bias, sink):
    b, s, _ = proj3d.shape
    nblk = s // BLOCK
    n_sub = ATTN_BLOCKS_PER_STEP if nblk % ATTN_BLOCKS_PER_STEP == 0 else 1
    nstep = nblk // n_sub
    in_specs, out_spec = _attention_specs(nblk, n_sub, lambda i: i)
    return pl.pallas_call(
        functools.partial(_attn_kernel, n_sub),
        grid=(b, nstep),
        in_specs=in_specs,
        out_specs=out_spec,
        out_shape=jax.ShapeDtypeStruct((b, s, ATTN_WIDTH), BF16),
        compiler_params=_params(2),
        name="window_attention",
    )(*_attention_operands(proj3d, bias, sink, n_sub))


def _attention_specs(nblk, n_sub, pos):
    def edge_spec(col, index):
        return pl.BlockSpec((1, BLOCK, KV_WIDTH), lambda bi, i: (bi, index(pos(i)), col))

    def kv_specs(col):
        return [edge_spec(col, lambda p: jnp.maximum(p * n_sub - 1, 0)),
                pl.BlockSpec((1, n_sub * BLOCK, KV_WIDTH), lambda bi, i: (bi, pos(i), col)),
                edge_spec(col, lambda p: jnp.minimum((p + 1) * n_sub, nblk - 1))]

    def bias_spec(sub):
        def variant(bi, i):
            first = (pos(i) * n_sub + sub == 0).astype(jnp.int32)
            last = (pos(i) * n_sub + sub == nblk - 1).astype(jnp.int32)
            return (first + 2 * last, 0, 0)
        return pl.BlockSpec((1, N_KV_HEADS * SPAN, Q_PER_KV * BLOCK), variant)

    in_specs = [pl.BlockSpec((1, n_sub * BLOCK, ATTN_WIDTH), lambda bi, i: (bi, pos(i), COL_Q)),
                *kv_specs(COL_K), *kv_specs(COL_V), _resident((1, N_HEADS * BLOCK)),
                *[bias_spec(sub) for sub in range(n_sub)]]
    return in_specs, pl.BlockSpec((1, n_sub * BLOCK, ATTN_WIDTH), lambda bi, i: (bi, pos(i), 0))


def _attention_operands(proj3d, bias, sink, n_sub):
    return [proj3d] * 7 + [sink] + [bias] * n_sub


def _split_hi_lo(v):
    hi = v.astype(BF16)
    return hi, (v - hi.astype(F32)).astype(BF16)


def _scan_tables(forward, dt_raw, dtb_ref, alog_ref, e2_ref):
    total_row = CHUNK - 1 if forward else 0
    dtv = _softplus(dt_raw + dtb_ref[...])
    adt = dtv * (-jnp.exp(alog_ref[...])) * LOG2_E
    ii = lax.broadcasted_iota(jnp.int32, (CHUNK, CHUNK), 0)
    jj = lax.broadcasted_iota(jnp.int32, (CHUNK, CHUNK), 1)
    scanned = (jj <= ii) if forward else (jj >= ii)
    tri = scanned.astype(BF16)
    a1 = adt.astype(BF16)
    r1 = adt - a1.astype(F32)
    a2 = r1.astype(BF16)
    a3 = (r1 - a2.astype(F32)).astype(BF16)
    pc = _dot(jnp.concatenate([tri, tri, tri], axis=1), jnp.concatenate([a1, a2, a3], axis=0))
    total = pc[total_row:total_row + 1, :]
    both = jnp.concatenate([jnp.exp2(pc), jnp.exp2(total - pc) * dtv], axis=0)
    hi, lo = _split_hi_lo(both)
    both_e = _dot(jnp.concatenate([hi, lo], axis=1), e2_ref[...])
    off_e = both_e[:CHUNK]
    w_e = both_e[CHUNK:]
    decay = off_e[total_row:total_row + 1, :]
    row_t = (pc - jnp.log(dtv) * LOG2_E).T
    return scanned, pc, row_t, off_e, w_e, decay


def _scan_group(g, lane0, scanned, pc, row_t, off_e, decay, b_t, c_bf, x_bf, xw_bf, state):
    gcols = slice(g * GROUP_WIDTH, (g + 1) * GROUP_WIDTH)
    gmat = _dot(c_bf, b_t)
    sg = state[g]
    y_off = _dot(c_bf, sg.astype(BF16)) * off_e[:, gcols]

    def decay_block(r):
        lane = lane0 + g * HEADS_PER_GROUP + r
        delta = pc[:, lane:lane + 1] - row_t[lane:lane + 1, :]
        return (gmat * jnp.exp2(jnp.where(scanned, delta, NEG_BIG))).astype(BF16)

    first_head = lax.broadcasted_iota(jnp.int32, (1, LANES), 1) < SSD_HEAD_DIM
    ys = []
    for q in range(HEADS_PER_GROUP // 2):
        x_pair = x_bf[:, q * LANES:(q + 1) * LANES]
        zero = jnp.zeros_like(x_pair)
        rhs = jnp.concatenate([jnp.where(first_head, x_pair, zero), jnp.where(first_head, zero, x_pair)],
                              axis=0)
        ys.append(_dot(jnp.concatenate([decay_block(2 * q), decay_block(2 * q + 1)], axis=1), rhs))
    state[g] = sg * decay[:, gcols] + _dot(b_t, xw_bf)
    return jnp.concatenate(ys, axis=-1) + y_off


def _reset_state_at_row_start(state):
    @pl.when(pl.program_id(1) == 0)
    def _():
        state[...] = jnp.zeros_like(state)


def _ssd_bwd_kernel(*refs):
    _reset_state_at_row_start(refs[-1])
    for _ in _ssd_bwd_steps(*refs):
        pass


def _ssd_bwd_steps(x_ref, bc_ref, dt_ref, dtb_ref, alog_ref, e2_ref, y_ref, bt_ref, state):
    order = list(reversed(range(x_ref.shape[1] // CHUNK)))
    tabs = {ci: _scan_tables(False, dt_ref[0, ci * CHUNK:(ci + 1) * CHUNK, :], dtb_ref, alog_ref, e2_ref)
            for ci in order}
    bc_half = BC_WIDTH // 2
    for ci in order:
        rows = slice(ci * CHUNK, (ci + 1) * CHUNK)
        scanned, pc, row_t, off_e, w_e, decay = tabs[ci]
        for g in range(SSD_GROUPS):
            gcols = slice(g * GROUP_WIDTH, (g + 1) * GROUP_WIDTH)
            x_bf = x_ref[0, rows, gcols]
            b_t = bc_ref[0, rows, g * D_STATE:(g + 1) * D_STATE].astype(F32).T.astype(BF16)
            bt_ref[0, ci * bc_half + g * D_STATE:ci * bc_half + (g + 1) * D_STATE, :] = b_t
            c_bf = bc_ref[0, rows, (SSD_GROUPS + g) * D_STATE:(SSD_GROUPS + g + 1) * D_STATE]
            y = _scan_group(g, SSD_HEADS, scanned, pc, row_t, off_e, decay, b_t, c_bf, x_bf,
                            x_bf * w_e[:, gcols].astype(BF16), state)
            y_ref[0, rows, gcols] = y.astype(BF16)
            yield


def _ssd_fwd_kernel(x_ref, c_ref, bt_ref, dt_ref, dtb_ref, alog_ref, e2_ref, yb_ref, z_ref, dskip_ref,
                    gn_ref, out_ref, state, ybuf):
    @pl.when(pl.program_id(1) == 0)
    def _():
        state[...] = jnp.zeros_like(state)

    order = list(range(x_ref.shape[1] // CHUNK))
    tabs = {ci: _scan_tables(True, dt_ref[0, ci * CHUNK:(ci + 1) * CHUNK, :], dtb_ref, alog_ref, e2_ref)
            for ci in order}
    bc_half = BC_WIDTH // 2
    for ci in order:
        rows = slice(ci * CHUNK, (ci + 1) * CHUNK)
        scanned, pc, row_t, off_e, w_e, decay = tabs[ci]
        for g in range(SSD_GROUPS):
            gcols = slice(g * GROUP_WIDTH, (g + 1) * GROUP_WIDTH)
            x_bf = x_ref[0, rows, gcols]
            xg = x_bf.astype(F32)
            b_t = bt_ref[0, ci * bc_half + g * D_STATE:ci * bc_half + (g + 1) * D_STATE, :]
            c_bf = c_ref[0, rows, g * D_STATE:(g + 1) * D_STATE]
            y = _scan_group(g, 0, scanned, pc, row_t, off_e, decay, b_t, c_bf, x_bf,
                            x_bf * w_e[:, gcols].astype(BF16), state)
            ybuf[rows, gcols] = y + yb_ref[0, rows, gcols].astype(F32) + xg * dskip_ref[:, gcols]
        gated = ybuf[rows, :] * z_ref[0, rows, :].astype(F32)
        out_ref[0, rows, :] = _rms(gated, gn_ref[...]).astype(BF16)


def _attn_ssd_bwd_kernel(n_sub, *refs):
    n_attn_in = 8 + n_sub
    n_ssd_in = 6
    o_ref, y_ref, bt_ref, state = refs[n_attn_in + n_ssd_in:]
    _reset_state_at_row_start(state)
    attn = _attn_steps(n_sub, *refs[:n_attn_in], o_ref)
    ssd = _ssd_bwd_steps(*refs[n_attn_in:n_attn_in + n_ssd_in], y_ref, bt_ref, state)
    for _ in itertools.zip_longest(attn, ssd):
        pass


def _mixer_calls(proj3d, dt3d, tables, w):
    b, s, _ = proj3d.shape
    nchunk = s // CHUNK
    per_step = SSD_CHUNKS_PER_STEP if nchunk % SSD_CHUNKS_PER_STEP == 0 else 1
    nstep = nchunk // per_step
    rows = per_step * CHUNK
    bc_half = BC_WIDTH // 2
    fused = per_step == ATTN_BLOCKS_PER_STEP and BLOCK == CHUNK

    def rev_spec(width, col=0):
        return pl.BlockSpec((1, rows, width), lambda bi, c: (bi, nstep - 1 - c, col))

    def chunk_spec(width, col=0):
        return pl.BlockSpec((1, rows, width), lambda bi, c: (bi, c, col))

    state = pltpu.VMEM((SSD_GROUPS, D_STATE, GROUP_WIDTH), F32)
    bwd_in_specs = [rev_spec(D_INNER, COL_XS), rev_spec(BC_WIDTH, COL_BC), rev_spec(DT_PAD),
                    _resident((1, DT_PAD)), _resident((1, DT_PAD)), _resident((2 * DT_PAD, D_INNER))]
    bwd_out_specs = [rev_spec(D_INNER),
                     pl.BlockSpec((1, per_step * bc_half, CHUNK), lambda bi, c: (bi, nstep - 1 - c, 0))]
    bwd_out_shape = [jax.ShapeDtypeStruct((b, s, D_INNER), BF16),
                     jax.ShapeDtypeStruct((b, nchunk * bc_half, CHUNK), BF16)]
    bwd_operands = [proj3d, proj3d, dt3d, w["dt_bias"], w["a_log"], tables["expand_bwd"]]
    if fused:
        attn_in_specs, attn_out_spec = _attention_specs(nchunk, per_step, lambda c: nstep - 1 - c)
        attn_o, y_bwd, bt = pl.pallas_call(
            functools.partial(_attn_ssd_bwd_kernel, per_step),
            grid=(b, nstep),
            in_specs=attn_in_specs + bwd_in_specs,
            out_specs=[attn_out_spec] + bwd_out_specs,
            out_shape=[jax.ShapeDtypeStruct((b, s, ATTN_WIDTH), BF16)] + bwd_out_shape,
            scratch_shapes=[state],
            compiler_params=_params(2),
            name="attention_ssd_backward",
        )(*_attention_operands(proj3d, tables["attn_bias"], w["attn_sink"], per_step), *bwd_operands)
    else:
        attn_o = _window_attention(proj3d, tables["attn_bias"], w["attn_sink"])
        y_bwd, bt = pl.pallas_call(
            _ssd_bwd_kernel,
            grid=(b, nstep),
            in_specs=bwd_in_specs,
            out_specs=bwd_out_specs,
            out_shape=bwd_out_shape,
            scratch_shapes=[state],
            compiler_params=_params(2),
            name="ssd_backward",
        )(*bwd_operands)

    ssd_o = pl.pallas_call(
        _ssd_fwd_kernel,
        grid=(b, nstep),
        in_specs=[chunk_spec(D_INNER, COL_XS), chunk_spec(bc_half, COL_BC * 2 + 1),
                  pl.BlockSpec((1, per_step * bc_half, CHUNK), lambda bi, c: (bi, c, 0)),
                  chunk_spec(DT_PAD),
                  _resident((1, DT_PAD)), _resident((1, DT_PAD)), _resident((2 * DT_PAD, D_INNER)),
                  chunk_spec(D_INNER), chunk_spec(D_INNER, COL_Z),
                  _resident((1, D_INNER)), _resident((1, D_INNER))],
        out_specs=chunk_spec(D_INNER),
        out_shape=jax.ShapeDtypeStruct((b, s, D_INNER), BF16),
        scratch_shapes=[state, pltpu.VMEM((rows, D_INNER), F32)],
        compiler_params=_params(2),
        name="ssd_forward",
    )(proj3d, proj3d, bt, dt3d, w["dt_bias"], w["a_log"], tables["expand_fwd"], y_bwd, proj3d,
      w["d_skip"], w["norm_ssd"])
    return attn_o, ssd_o


def _mem_kv_kernel(mem_ref, g_ref, w_ref, o_ref):
    h = _rms(mem_ref[0], g_ref[...]).astype(BF16)
    o_ref[0] = _dot(h, w_ref[...]).astype(BF16)


def _mem_kv(mem, g, w_kv):
    b = mem.shape[0]
    return pl.pallas_call(
        _mem_kv_kernel,
        grid=(b,),
        in_specs=[pl.BlockSpec((1, MEM_LEN, D_MODEL), lambda i: (i, 0, 0)),
                  _resident((1, D_MODEL)), _resident((D_MODEL, 2 * D_MODEL))],
        out_specs=pl.BlockSpec((1, MEM_LEN, 2 * D_MODEL), lambda i: (i, 0, 0)),
        out_shape=jax.ShapeDtypeStruct((b, MEM_LEN, 2 * D_MODEL), BF16),
        compiler_params=_params(1),
        name="memory_kv",
    )(mem, g, w_kv)


def _merge_cross_kernel(ssd_ref, att_ref, gate_ref, x_ref, kv_ref, wpa_ref, wps_ref, wout_ref,
                        gc_ref, wq_ref, wo_ref, o_ref):
    gates = _sigmoid(gate_ref[...].astype(F32))
    mixed = (gates[:, :D_MODEL] * _dot(att_ref[...], wpa_ref[...])
             + gates[:, D_MODEL:] * _dot(ssd_ref[...], wps_ref[...]))
    x1 = x_ref[...] + _dot(mixed.astype(BF16), wout_ref[...])
    h = _rms(x1, gc_ref[...]).astype(BF16)
    q = (_dot(h, wq_ref[...]) * CROSS_HEAD_DIM ** -0.5).astype(BF16)
    outs = []
    for hh in range(CROSS_HEADS):
        hs = slice(hh * CROSS_HEAD_DIM, (hh + 1) * CROSS_HEAD_DIM)
        vs = slice(D_MODEL + hh * CROSS_HEAD_DIM, D_MODEL + (hh + 1) * CROSS_HEAD_DIM)
        s = _dot_nt(q[:, hs], kv_ref[0, :, hs])
        e = jnp.exp(s - jnp.max(s, axis=-1, keepdims=True))
        den = jnp.sum(e, axis=-1, keepdims=True)
        outs.append(_dot(e.astype(BF16), kv_ref[0, :, vs]) * (1.0 / den))
    o = jnp.concatenate(outs, axis=-1).astype(BF16)
    o_ref[...] = x1 + _dot(o, wo_ref[...])


def _merge_cross(ssd_o, attn_o, proj2d, x2d, kv, w, tm, seq):
    m = x2d.shape[0]
    tiles_per_seq = seq // tm
    return pl.pallas_call(
        _merge_cross_kernel,
        grid=(m // tm,),
        in_specs=[pl.BlockSpec((tm, D_INNER), lambda i: (i, 0)),
                  pl.BlockSpec((tm, ATTN_WIDTH), lambda i: (i, 0)),
                  pl.BlockSpec((tm, 2 * D_MODEL), lambda i: (i, COL_GATES)),
                  pl.BlockSpec((tm, D_MODEL), lambda i: (i, 0)),
                  pl.BlockSpec((1, MEM_LEN, 2 * D_MODEL), lambda i: (i // tiles_per_seq, 0, 0)),
                  _resident((ATTN_WIDTH, D_MODEL)), _resident((D_INNER, D_MODEL)),
                  _resident((D_MODEL, D_MODEL)), _resident((1, D_MODEL)),
                  _resident((D_MODEL, D_MODEL)), _resident((D_MODEL, D_MODEL))],
        out_specs=pl.BlockSpec((tm, D_MODEL), lambda i: (i, 0)),
        out_shape=jax.ShapeDtypeStruct((m, D_MODEL), F32),
        compiler_params=_params(1),
        name="merge_cross",
    )(ssd_o, attn_o, proj2d, x2d, kv, w["w_proj_attn"], w["w_proj_ssd"], w["w_out"],
      w["norm_cross"], w["w_q_cross"], w["w_o_cross"])


def _ffn_chunks():
    chunks, f0 = [], 0
    while f0 < D_FF:
        fw = min(FFN_CHUNK, D_FF - f0)
        chunks.append((f0, fw))
        f0 += fw
    return chunks


def _ffn_kernel(final_norm, x_ref, g_ref, wgu_ref, wd_ref, gf_ref, o_ref, acc_ref):
    x = x_ref[...]
    h = _rms(x, g_ref[...]).astype(BF16)
    for n, (f0, fw) in enumerate(_ffn_chunks()):
        gate = _dot(h, wgu_ref[:, f0:f0 + fw])
        up = _dot(h, wgu_ref[:, D_FF + f0:D_FF + f0 + fw])
        part = _dot((_silu(gate) * up).astype(BF16), wd_ref[f0:f0 + fw, :])
        if n == 0:
            acc_ref[...] = x + part
        else:
            acc_ref[...] += part
    y = acc_ref[...]
    o_ref[...] = _rms(y, gf_ref[...]) if final_norm else y


def _ffn(x2d, g, w_gate_up, w_down, g_final, final_norm, tm):
    m = x2d.shape[0]
    return pl.pallas_call(
        functools.partial(_ffn_kernel, final_norm),
        grid=(m // tm,),
        in_specs=[pl.BlockSpec((tm, D_MODEL), lambda i: (i, 0)),
                  _resident((1, D_MODEL)),
                  _resident((D_MODEL, 2 * D_FF)), _resident((D_FF, D_MODEL)),
                  _resident((1, D_MODEL))],
        out_specs=pl.BlockSpec((tm, D_MODEL), lambda i: (i, 0)),
        out_shape=jax.ShapeDtypeStruct((m, D_MODEL), F32),
        scratch_shapes=[pltpu.VMEM((tm, D_MODEL), F32)],
        compiler_params=_params(1),
        name="ffn_final" if final_norm else "ffn",
    )(x2d, g, w_gate_up, w_down, g_final)


def _t5_bucket(rel):
    nb = NUM_BUCKETS // 2
    max_exact = nb // 2
    ret = jnp.where(rel > 0, nb, 0)
    n = jnp.abs(rel)
    nf = jnp.maximum(n, 1).astype(F32)
    large = max_exact + (jnp.log(nf / max_exact) / math.log(MAX_DISTANCE / max_exact)
                         * (nb - max_exact)).astype(jnp.int32)
    large = jnp.minimum(large, nb - 1)
    return ret + jnp.where(n < max_exact, n, large)


def _attention_bias(rel_bias):
    rel = jnp.arange(SPAN)[None, :] - WINDOW - jnp.arange(BLOCK)[:, None]
    one_hot = (_t5_bucket(rel)[None] == jnp.arange(NUM_BUCKETS)[:, None, None]).astype(F32)
    bias = jnp.einsum("bh,bqk->hqk", rel_bias.astype(F32), one_hot, precision=lax.Precision.HIGHEST)
    bias = jnp.where((jnp.abs(rel) <= WINDOW)[None], bias, NEG_BIG)
    col = jnp.arange(SPAN)
    variants = []
    for first, last in ((False, False), (True, False), (False, True), (True, True)):
        pad = (first & (col < WINDOW)) | (last & (col >= WINDOW + BLOCK))
        masked = jnp.where(pad[None, None, :], NEG_BIG, bias).reshape(N_KV_HEADS, Q_PER_KV, BLOCK, SPAN)
        variants.append(jnp.transpose(masked, (0, 3, 1, 2)).reshape(N_KV_HEADS * SPAN, Q_PER_KV * BLOCK))
    return jnp.stack(variants)


def _head_expand_matrix(lane0):
    e = np.zeros((2 * DT_PAD, D_INNER), np.float32)
    for h in range(SSD_HEADS):
        e[lane0 + h, h * SSD_HEAD_DIM:(h + 1) * SSD_HEAD_DIM] = 1.0
        e[DT_PAD + lane0 + h, h * SSD_HEAD_DIM:(h + 1) * SSD_HEAD_DIM] = 1.0
    return jnp.asarray(e, BF16)


def _tables(rel_bias):
    return {"attn_bias": _attention_bias(rel_bias), "expand_fwd": _head_expand_matrix(0),
            "expand_bwd": _head_expand_matrix(SSD_HEADS)}


def _prepare_layer(l, p):
    w_in = p["w_in"][l]
    o_q, o_k, o_v = 0, ATTN_WIDTH, ATTN_WIDTH + KV_WIDTH
    o_z = ATTN_WIDTH + 2 * KV_WIDTH
    o_xs = o_z + D_INNER
    o_bc = o_xs + D_INNER
    o_dt = o_bc + BC_WIDTH
    o_g = o_dt + 2 * SSD_HEADS
    w_main = jnp.concatenate([w_in[:, o_z:o_xs], w_in[:, o_xs:o_bc], w_in[:, o_g:o_g + 2 * D_MODEL],
                              w_in[:, o_q:o_k], w_in[:, o_bc:o_dt], w_in[:, o_k:o_v],
                              w_in[:, o_v:o_z]], axis=1).astype(BF16)
    w_dt = jnp.pad(w_in[:, o_dt:o_g], ((0, 0), (0, DT_PAD - 2 * SSD_HEADS))).astype(BF16)
    pad_dt = lambda a: jnp.pad(a.reshape(1, 2 * SSD_HEADS), ((0, 0), (0, DT_PAD - 2 * SSD_HEADS)))
    row = lambda a: a.reshape(1, -1).astype(F32)
    return {
        "norm_mix": row(p["norm_mix"][l]), "w_main": w_main, "w_dt": w_dt,
        "attn_sink": jnp.repeat(p["attn_sink"][l].astype(F32), BLOCK).reshape(1, N_HEADS * BLOCK),
        "conv_w": p["conv_w"][l].astype(F32), "conv_b": row(p["conv_b"][l]),
        "dt_bias": pad_dt(p["dt_bias"][l]), "a_log": pad_dt(p["a_log"][l]),
        "d_skip": row(jnp.repeat(p["d_skip"][l], SSD_HEAD_DIM)),
        "norm_ssd": row(p["norm_ssd"][l]),
        "w_proj_attn": p["w_proj_attn"][l].astype(BF16), "w_proj_ssd": p["w_proj_ssd"][l].astype(BF16),
        "w_out": p["w_out"][l].astype(BF16), "norm_cross": row(p["norm_cross"][l]),
        "norm_mem": row(p["norm_mem"][l]), "w_q_cross": p["w_q_cross"][l].astype(BF16),
        "w_kv_cross": p["w_kv_cross"][l].astype(BF16), "w_o_cross": p["w_o_cross"][l].astype(BF16),
        "norm_ffn": row(p["norm_ffn"][l]), "w_gate_up": p["w_gate_up"][l].astype(BF16),
        "w_down": p["w_down"][l].astype(BF16),
    }


def _encoder(x, mem, layers, tables, norm_final):
    b, s, _ = x.shape
    m = b * s
    tm = min(ROW_TILE, s)
    x2d = x.reshape(m, D_MODEL)
    for l, w in enumerate(layers):
        proj, dt = _in_proj(x2d, w, tm, s)
        proj3d = proj.reshape(b, s, PROJ_WIDTH)
        attn_o, ssd_o = _mixer_calls(proj3d, dt.reshape(b, s, DT_PAD), tables, w)
        kv = _mem_kv(mem, w["norm_mem"], w["w_kv_cross"])
        x2d = _merge_cross(ssd_o.reshape(m, D_INNER), attn_o.reshape(m, ATTN_WIDTH), proj, x2d, kv, w,
                           tm, s)
        x2d = _ffn(x2d, w["norm_ffn"], w["w_gate_up"], w["w_down"], norm_final,
                   l == len(layers) - 1, tm)
    return x2d.reshape(b, s, D_MODEL)


def kernel(x_prompt, x_sample, mem_prompt, mem_sample, rel_bias, norm_mix, w_in, attn_sink, conv_w,
           conv_b, dt_bias, a_log, d_skip, norm_ssd, w_proj_attn, w_proj_ssd, w_out, norm_cross,
           norm_mem, w_q_cross, w_kv_cross, w_o_cross, norm_ffn, w_gate_up, w_down, norm_final):
    p = dict(norm_mix=norm_mix, w_in=w_in, attn_sink=attn_sink, conv_w=conv_w, conv_b=conv_b,
             dt_bias=dt_bias, a_log=a_log, d_skip=d_skip, norm_ssd=norm_ssd, w_proj_attn=w_proj_attn,
             w_proj_ssd=w_proj_ssd, w_out=w_out, norm_cross=norm_cross, norm_mem=norm_mem,
             w_q_cross=w_q_cross, w_kv_cross=w_kv_cross, w_o_cross=w_o_cross, norm_ffn=norm_ffn,
             w_gate_up=w_gate_up, w_down=w_down)
    layers = [_prepare_layer(l, p) for l in range(norm_mix.shape[0])]
    tables = _tables(rel_bias)
    g_final = norm_final.reshape(1, D_MODEL).astype(F32)
    y_prompt = _encoder(x_prompt, mem_prompt, layers, tables, g_final)
    y_sample = _encoder(x_sample, mem_sample, layers, tables, g_final)
    return (y_prompt, y_sample)
```
